```python
import jax, jax.numpy as jnp
from jax import lax
import numpy as np

D_MODEL = 2048
BATCH = 2
SEQ = 4096
DEPTH = 1
DEC_BATCH = 32
DEC_SEQ = 4
PAST_LEN = 16384
PAGE_SIZE = 128

HEAD_DIM = 128
SB_HEADS = D_MODEL // (2 * HEAD_DIM)
RET_HEADS = D_MODEL // (2 * HEAD_DIM)
SB_WIDTH = SB_HEADS * HEAD_DIM
RET_WIDTH = RET_HEADS * HEAD_DIM
MIX_WIDTH = SB_WIDTH + RET_WIDTH
IN_WIDTH = 3 * SB_WIDTH + 4 * RET_WIDTH
SB_BLOCK = 128
SB_BIAS_INIT = -8.0
RET_CHUNK = 128
ROPE_BASE = 10000.0
NORM_EPS = 1e-6
PEER_HEADS = 8
PEER_KEYS = 128
PEER_EXPERTS = PEER_KEYS * PEER_KEYS
PEER_TOPK = 16
PEER_QDIM = 256
PEER_HALF = PEER_QDIM // 2
PEER_BLOCK = 128

kernel_name = 'hybrid_stickbreak_retention_peer_step'


def rms_norm(x, gain):
    xf = x.astype(jnp.float32)
    y = xf * lax.rsqrt(jnp.mean(xf * xf, axis=-1, keepdims=True) + NORM_EPS) * gain
    return y.astype(x.dtype)


def group_norm(x, gain):
    xf = x.astype(jnp.float32)
    mu = jnp.mean(xf, axis=-1, keepdims=True)
    var = jnp.mean(jnp.square(xf - mu), axis=-1, keepdims=True)
    return (xf - mu) * lax.rsqrt(var + NORM_EPS) * gain


def rope(x, pos):
    half = x.shape[-1] // 2
    freqs = ROPE_BASE ** (-jnp.arange(half, dtype=jnp.float32) / half)
    ang = pos.astype(jnp.float32)[:, None] * freqs[None, :]
    cos = jnp.cos(ang)[None, :, None, :]
    sin = jnp.sin(ang)[None, :, None, :]
    xf = x.astype(jnp.float32)
    x1, x2 = xf[..., :half], xf[..., half:]
    return jnp.concatenate([x1 * cos - x2 * sin, x1 * sin + x2 * cos], axis=-1)


def project(n, pos, w_in, q_gain, k_gain):
    B, L, _ = n.shape
    proj = n @ w_in
    cuts = [SB_WIDTH, 2 * SB_WIDTH, 3 * SB_WIDTH, 3 * SB_WIDTH + RET_WIDTH,
            3 * SB_WIDTH + 2 * RET_WIDTH, 3 * SB_WIDTH + 3 * RET_WIDTH]
    qa, ka, va, qb, kb, vb, gb = jnp.split(proj, cuts, axis=-1)
    heads = lambda t: t.reshape(B, L, -1, HEAD_DIM)
    q_a = rms_norm(heads(qa), q_gain)
    k_a = rms_norm(heads(ka), k_gain)
    v_a = heads(va)
    q_b = rope(heads(qb), pos)
    k_b = rope(heads(kb), pos) * HEAD_DIM ** -0.5
    v_b = heads(vb).astype(jnp.float32)
    return q_a, k_a, v_a, q_b, k_b, v_b, gb


def sb_block(q, k, v, bias, q_pos, k_pos):
    z = jnp.einsum('bqhd,bkhd->bhqk', q, k).astype(jnp.float32) * HEAD_DIM ** -0.5
    z = z + bias.astype(jnp.float32)[None, :, None, None]
    causal = k_pos[None, :] < q_pos[:, None]
    sp = jnp.where(causal, jax.nn.softplus(z), 0.0)
    after = lax.cumsum(sp, axis=3, reverse=True) - sp
    w = jnp.where(causal, jnp.exp(jax.nn.log_sigmoid(z) - after), 0.0)
    return jnp.einsum('bhqk,bkhd->bqhd', w.astype(v.dtype), v)


def stick_breaking(q, k, v, bias):
    B, L, H, Dh = q.shape
    P = k.shape[1] - L
    k_pos = jnp.arange(P + L)
    q_pos = P + jnp.arange(L)
    if L > SB_BLOCK and L % SB_BLOCK == 0:
        nb = L // SB_BLOCK
        qb = jnp.moveaxis(q.reshape(B, nb, SB_BLOCK, H, Dh), 1, 0)
        pb = q_pos.reshape(nb, SB_BLOCK)
        ob = lax.map(lambda a: sb_block(a[0], k, v, bias, a[1], k_pos), (qb, pb))
        return jnp.moveaxis(ob, 0, 1).reshape(B, L, H, Dh)
    return sb_block(q, k, v, bias, q_pos, k_pos)


def retention_chunk(S, q, k, v, log_g):
    L = q.shape[1]
    i = jnp.arange(L, dtype=jnp.float32)
    dist = i[:, None] - i[None, :]
    decay = jnp.where(dist >= 0, jnp.exp(jnp.maximum(dist, 0.0)[None] * log_g[:, None, None]), 0.0)
    scores = jnp.einsum('blhd,bmhd->bhlm', q, k) * decay[None]
    o = jnp.einsum('bhlm,bmhv->blhv', scores, v)
    o = o + jnp.einsum('blhd,bhdv->blhv', q, S) * jnp.exp((i + 1.0)[:, None] * log_g[None, :])[None, :, :, None]
    k_tail = k * jnp.exp((L - 1.0 - i)[:, None] * log_g[None, :])[None, :, :, None]
    S_new = S * jnp.exp(L * log_g)[None, :, None, None] + jnp.einsum('blhd,blhv->bhdv', k_tail, v)
    return S_new, o


def retention(q, k, v, S0):
    log_g = jnp.log1p(-jnp.power(2.0, -5.0 - jnp.arange(RET_HEADS, dtype=jnp.float32)))
    B, L, H, Dh = q.shape
    if L > RET_CHUNK and L % RET_CHUNK == 0:
        nc = L // RET_CHUNK
        split = lambda t: jnp.moveaxis(t.reshape(B, nc, RET_CHUNK, H, t.shape[-1]), 1, 0)
        S, oc = lax.scan(lambda s, c: retention_chunk(s, c[0], c[1], c[2], log_g), S0,
                         (split(q), split(k), split(v)))
        return S, jnp.moveaxis(oc, 0, 1).reshape(B, L, H, -1)
    return retention_chunk(S0, q, k, v, log_g)


def mixing(n, pos, past_k, past_v, S0, w_in, q_gain, k_gain, sb_bias, sb_gain, ret_gain, w_out):
    B, L, _ = n.shape
    q_a, k_a, v_a, q_b, k_b, v_b, g_b = project(n, pos, w_in, q_gain, k_gain)
    if past_k is None:
        k_all, v_all = k_a, v_a
    else:
        k_all = jnp.concatenate([past_k.astype(k_a.dtype), k_a], axis=1)
        v_all = jnp.concatenate([past_v.astype(v_a.dtype), v_a], axis=1)
    o_a = rms_norm(stick_breaking(q_a, k_all, v_all, sb_bias), sb_gain)
    S_new, o_b = retention(q_b, k_b, v_b, S0.astype(jnp.float32))
    o_b = group_norm(o_b, ret_gain) * jax.nn.silu(g_b.astype(jnp.float32)).reshape(B, L, RET_HEADS, HEAD_DIM)
    mix = jnp.concatenate([o_a.reshape(B, L, SB_WIDTH), o_b.reshape(B, L, RET_WIDTH).astype(n.dtype)], axis=-1)
    return mix @ w_out, k_a, v_a, S_new


def peer_block(xb, w_q, subkeys, u, v):
    T = xb.shape[0]
    q = (xb @ w_q).reshape(T, PEER_HEADS, 2, PEER_HALF)
    s = jnp.einsum('thcd,hcnd->thcn', q, subkeys).astype(jnp.float32)
    s1, i1 = lax.top_k(s[:, :, 0], PEER_TOPK)
    s2, i2 = lax.top_k(s[:, :, 1], PEER_TOPK)
    cand_s = (s1[..., :, None] + s2[..., None, :]).reshape(T, PEER_HEADS, PEER_TOPK * PEER_TOPK)
    cand_i = (i1[..., :, None] * PEER_KEYS + i2[..., None, :]).reshape(T, PEER_HEADS, PEER_TOPK * PEER_TOPK)
    top_s, sel = lax.top_k(cand_s, PEER_TOPK)
    idx = jnp.take_along_axis(cand_i, sel, axis=-1)
    gate = jax.nn.softmax(top_s, axis=-1).astype(xb.dtype)
    act = jax.nn.gelu(jnp.einsum('td,thkd->thk', xb, u[idx])) * gate
    return jnp.einsum('thk,thkd->td', act, v[idx])


def peer(xn, w_q, subkeys, u, v):
    B, L, D = xn.shape
    T = B * L
    nb = -(-T // PEER_BLOCK)
    flat = jnp.pad(xn.reshape(T, D), ((0, nb * PEER_BLOCK - T), (0, 0)))
    out = lax.map(lambda xb: peer_block(xb, w_q, subkeys, u, v), flat.reshape(nb, PEER_BLOCK, D))
    return out.reshape(nb * PEER_BLOCK, D)[:T].reshape(B, L, D)


def setup_inputs(seed: int = 0) -> dict:
    key = jax.random.key(seed)
    ks = jax.random.split(key, 20)
    n_pages = PAST_LEN // PAGE_SIZE
    used = DEC_BATCH * n_pages
    n_pool = used + max(1, used // 4)
    nrm = lambda k, shape, scale=1.0: scale * jax.random.normal(k, shape, jnp.float32)
    gain = lambda k, shape: 1.0 + 0.02 * jax.random.normal(k, shape, jnp.float32)
    page_table = jax.random.permutation(ks[5], n_pool)[:used].reshape(DEC_BATCH, n_pages).astype(jnp.int32)
    return {
        'x_prompt': nrm(ks[0], (BATCH, SEQ, D_MODEL)),
        'x_sample': nrm(ks[1], (DEC_BATCH, DEC_SEQ, D_MODEL)),
        'cache_k': nrm(ks[2], (DEPTH, n_pool, PAGE_SIZE, SB_HEADS, HEAD_DIM)),
        'cache_v': nrm(ks[3], (DEPTH, n_pool, PAGE_SIZE, SB_HEADS, HEAD_DIM)),
        'state_ret': nrm(ks[4], (DEPTH, DEC_BATCH, RET_HEADS, HEAD_DIM, HEAD_DIM), 0.5),
        'page_table': page_table,
        'norm1_gain': gain(ks[6], (DEPTH, D_MODEL)),
        'w_in': nrm(ks[7], (DEPTH, D_MODEL, IN_WIDTH), D_MODEL ** -0.5),
        'q_norm_gain': gain(ks[8], (DEPTH, SB_HEADS, HEAD_DIM)),
        'k_norm_gain': gain(ks[9], (DEPTH, SB_HEADS, HEAD_DIM)),
        'sb_bias': SB_BIAS_INIT + 0.5 * jax.random.normal(ks[18], (DEPTH, SB_HEADS), jnp.float32),
        'sb_out_gain': gain(ks[10], (DEPTH, SB_HEADS, HEAD_DIM)),
        'ret_out_gain': gain(ks[11], (DEPTH, RET_HEADS, HEAD_DIM)),
        'w_out': nrm(ks[12], (DEPTH, MIX_WIDTH, D_MODEL), MIX_WIDTH ** -0.5),
        'norm2_gain': gain(ks[13], (DEPTH, D_MODEL)),
        'peer_w_q': nrm(ks[14], (DEPTH, D_MODEL, PEER_HEADS * PEER_QDIM), D_MODEL ** -0.5),
        'peer_subkeys': nrm(ks[15], (DEPTH, PEER_HEADS, 2, PEER_KEYS, PEER_HALF), PEER_HALF ** -0.5),
        'peer_u': nrm(ks[16], (DEPTH, PEER_EXPERTS, D_MODEL), D_MODEL ** -0.5),
        'peer_v': nrm(ks[17], (DEPTH, PEER_EXPERTS, D_MODEL), PEER_HEADS ** -0.5),
    }


def reference(x_prompt, x_sample, cache_k, cache_v, state_ret, page_table, norm1_gain, w_in,
              q_norm_gain, k_norm_gain, sb_bias, sb_out_gain, ret_out_gain, w_out, norm2_gain,
              peer_w_q, peer_subkeys, peer_u, peer_v):
    dec_b, n_pages = page_table.shape
    past = n_pages * cache_k.shape[2]
    pos_p = jnp.arange(x_prompt.shape[1])
    pos_s = past + jnp.arange(x_sample.shape[1])
    hp, hs = x_prompt, x_sample
    kp, vp, rp, ksm, vsm, rsm = [], [], [], [], [], []
    for l in range(DEPTH):
        mix_w = (w_in[l], q_norm_gain[l], k_norm_gain[l], sb_bias[l], sb_out_gain[l], ret_out_gain[l], w_out[l])
        peer_w = (peer_w_q[l], peer_subkeys[l], peer_u[l], peer_v[l])
        S0 = jnp.zeros((hp.shape[0], RET_HEADS, HEAD_DIM, HEAD_DIM), jnp.float32)
        m, k_new, v_new, S_new = mixing(rms_norm(hp, norm1_gain[l]), pos_p, None, None, S0, *mix_w)
        hp = hp + m
        hp = hp + peer(rms_norm(hp, norm2_gain[l]), *peer_w)
        kp.append(k_new); vp.append(v_new); rp.append(S_new)
        past_k = cache_k[l][page_table].reshape(dec_b, past, SB_HEADS, HEAD_DIM)
        past_v = cache_v[l][page_table].reshape(dec_b, past, SB_HEADS, HEAD_DIM)
        m, k_new, v_new, S_new = mixing(rms_norm(hs, norm1_gain[l]), pos_s, past_k, past_v, state_ret[l], *mix_w)
        hs = hs + m
        hs = hs + peer(rms_norm(hs, norm2_gain[l]), *peer_w)
        ksm.append(k_new); vsm.append(v_new); rsm.append(S_new)
    return (hp, hs, jnp.stack(kp), jnp.stack(vp), jnp.stack(rp), jnp.stack(ksm), jnp.stack(vsm), jnp.stack(rsm))
```

```python
import functools

import jax
import jax.numpy as jnp
from jax import lax
from jax.experimental import pallas as pl
from jax.experimental.pallas import tpu as pltpu

HEAD_DIM = 128
NORM_EPS = 1e-6
ROPE_BASE = 10000.0
PEER_HEADS = 8
PEER_KEYS = 128
PEER_TOPK = 16
PEER_HALF = 128
QK_SCALE = HEAD_DIM ** -0.5

_F32 = jnp.float32
_BF16 = jnp.bfloat16
_NEG_INF = float("-inf")
_V7X_VMEM_LIMIT_BYTES = 56 * 1024 * 1024

_NT = (((1,), (1,)), ((), ()))


def _params(sem):
    return pltpu.CompilerParams(dimension_semantics=sem, vmem_limit_bytes=_V7X_VMEM_LIMIT_BYTES)


def _tile(n, cap, mult):
    best = None
    for d in range(mult, min(n, cap) + 1, mult):
        if n % d == 0:
            best = d
    assert best is not None, (n, cap, mult)
    return best


def _softplus(z):
    return jnp.maximum(z, 0.0) + jnp.log1p(jnp.exp(-jnp.abs(z)))


def _strict_lower_ones(n):
    row = lax.broadcasted_iota(jnp.int32, (n, n), 0)
    col = lax.broadcasted_iota(jnp.int32, (n, n), 1)
    return jnp.where(row > col, 1.0, 0.0).astype(_BF16)


def _inproj_body(x_ref, g_ref, w_ref, cos_ref, sin_ref, qg_ref, kg_ref, o_ref, xn_ref, *, n_heads):
    j = pl.program_id(1)

    @pl.when(j == 0)
    def _():
        x = x_ref[...]
        ms = jnp.mean(x * x, axis=-1, keepdims=True)
        xn_ref[...] = (x * lax.rsqrt(ms + NORM_EPS) * g_ref[...]).astype(_BF16)

    acc = jnp.dot(xn_ref[...], w_ref[...], preferred_element_type=_F32)

    def per_head(fn):
        for h in range(n_heads):
            sl = slice(h * HEAD_DIM, (h + 1) * HEAD_DIM)
            o_ref[:, sl] = fn(acc[:, sl], h)

    def rms(a, gain):
        return a * lax.rsqrt(jnp.mean(a * a, axis=-1, keepdims=True) + NORM_EPS) * gain

    def rope(a):
        return a * cos_ref[...] + pltpu.roll(a, HEAD_DIM // 2, 1) * sin_ref[...]

    @pl.when(j == 0)
    def _():
        per_head(lambda a, h: rms(a, qg_ref[h:h + 1, :]) * QK_SCALE)

    @pl.when(j == 1)
    def _():
        per_head(lambda a, h: rms(a, kg_ref[h:h + 1, :]))

    @pl.when(j == 3)
    def _():
        per_head(lambda a, h: rope(a))

    @pl.when(j == 4)
    def _():
        per_head(lambda a, h: rope(a) * QK_SCALE)

    @pl.when((j == 2) | (j == 5) | (j == 6))
    def _():
        o_ref[...] = acc


def _inproj(x, gain, w_bf16, cos_t, sin_t, q_gain, k_gain, n_heads):
    T, D = x.shape
    width = n_heads * HEAD_DIM
    n_seg = w_bf16.shape[1] // width
    tm = _tile(T, 640, 8)
    return pl.pallas_call(
        functools.partial(_inproj_body, n_heads=n_heads),
        out_shape=jax.ShapeDtypeStruct((T, n_seg * width), _F32),
        grid=(T // tm, n_seg),
        in_specs=[
            pl.BlockSpec((tm, D), lambda i, j: (i, 0)),
            pl.BlockSpec((1, D), lambda i, j: (0, 0)),
            pl.BlockSpec((D, width), lambda i, j: (0, j)),
            pl.BlockSpec((tm, HEAD_DIM), lambda i, j: (i, 0)),
            pl.BlockSpec((tm, HEAD_DIM), lambda i, j: (i, 0)),
            pl.BlockSpec((n_heads, HEAD_DIM), lambda i, j: (0, 0)),
            pl.BlockSpec((n_heads, HEAD_DIM), lambda i, j: (0, 0)),
        ],
        out_specs=pl.BlockSpec((tm, width), lambda i, j: (i, j)),
        scratch_shapes=[pltpu.VMEM((tm, D), _BF16)],
        compiler_params=_params(("arbitrary", "arbitrary")),
        name="inproj",
    )(x, gain, w_bf16, cos_t, sin_t, q_gain, k_gain)


def _sb_block(q, ks, vs, bias, tri, carry, acc, mask):
    z = lax.dot_general(q, ks, _NT, preferred_element_type=_F32) + bias
    sp = _softplus(z)
    if mask is not None:
        sp = jnp.where(mask, sp, 0.0)
    later = jnp.dot(sp.astype(_BF16), tri, preferred_element_type=_F32)
    w = jnp.exp(z - sp - (carry + later))
    if mask is not None:
        w = jnp.where(mask, w, 0.0)
    acc = acc + jnp.dot(w.astype(_BF16), vs, preferred_element_type=_F32)
    carry = carry + jnp.sum(sp, axis=1, keepdims=True)
    return carry, acc


def _sb_prompt_body(bias_ref, q_ref, k_ref, v_ref, gain_ref, o_ref, *, blk):
    h = pl.program_id(1)
    i = pl.program_id(2)
    bias = bias_ref[h]
    q = q_ref[...].astype(_BF16)
    tri = _strict_lower_ones(blk)
    row = lax.broadcasted_iota(jnp.int32, (blk, blk), 0)
    col = lax.broadcasted_iota(jnp.int32, (blk, blk), 1)

    def step(kb, carry, acc, mask):
        start = pl.multiple_of(kb * blk, blk)
        ks = k_ref[pl.ds(start, blk), :].astype(_BF16)
        vs = v_ref[pl.ds(start, blk), :].astype(_BF16)
        return _sb_block(q, ks, vs, bias, tri, carry, acc, mask)

    carry = jnp.zeros((blk, 1), _F32)
    acc = jnp.zeros((blk, HEAD_DIM), _F32)
    carry, acc = step(i, carry, acc, col < row)
    carry, acc = lax.fori_loop(0, i, lambda s, c: step(i - 1 - s, c[0], c[1], None), (carry, acc))
    ms = jnp.mean(acc * acc, axis=-1, keepdims=True)
    o_ref[...] = (acc * lax.rsqrt(ms + NORM_EPS) * gain_ref[...]).astype(o_ref.dtype)


def _sb_prompt(proj, sb_bias, sb_gain, batch, seq, n_heads):
    blk = _tile(seq, 256, 128)
    nq = seq // blk
    return pl.pallas_call(
        functools.partial(_sb_prompt_body, blk=blk),
        out_shape=jax.ShapeDtypeStruct((batch * seq, n_heads * HEAD_DIM), _BF16),
        grid=(batch, n_heads, nq),
        in_specs=[
            pl.BlockSpec(memory_space=pltpu.SMEM),
            pl.BlockSpec((blk, HEAD_DIM), lambda b, h, i: (b * nq + i, h)),
            pl.BlockSpec((seq, HEAD_DIM), lambda b, h, i: (b, n_heads + h)),
            pl.BlockSpec((seq, HEAD_DIM), lambda b, h, i: (b, 2 * n_heads + h)),
            pl.BlockSpec((None, 1, HEAD_DIM), lambda b, h, i: (h, 0, 0)),
        ],
        out_specs=pl.BlockSpec((blk, HEAD_DIM), lambda b, h, i: (b * nq + i, h)),
        compiler_params=_params(("arbitrary", "arbitrary", "arbitrary")),
        name="sb_prompt",
    )(sb_bias, proj, proj, proj, sb_gain.reshape(n_heads, 1, HEAD_DIM))


def _sb_decode_body(pt_ref, qbd_ref, bias_ref, tq_ref, knew_ref, vnew_ref, gain_ref, *rest,
                    pages_per_step, n_new, n_heads):
    del pt_ref
    kv_refs = rest[:2 * pages_per_step]
    o_ref, carry_ref, acc_ref = rest[2 * pages_per_step:]
    s = pl.program_id(1)
    n_rows, page = qbd_ref.shape[0], knew_ref.shape[0]
    qbd = qbd_ref[...]
    bias = bias_ref[...]
    tri = _strict_lower_ones(page)

    def visit(k_page, v_page, mask):
        carry, acc = _sb_block(qbd, k_page.astype(_BF16), v_page.astype(_BF16), bias, tri,
                               carry_ref[...], acc_ref[...], mask)
        carry_ref[...] = carry
        acc_ref[...] = acc

    @pl.when(s == 0)
    def _():
        carry_ref[...] = jnp.zeros_like(carry_ref)
        acc_ref[...] = jnp.zeros_like(acc_ref)
        key = lax.broadcasted_iota(jnp.int32, (n_rows, page), 1)
        visit(knew_ref[...], vnew_ref[...], key < tq_ref[...])

    for r in range(pages_per_step):
        visit(kv_refs[2 * r][...], kv_refs[2 * r + 1][...], None)

    @pl.when(s == pl.num_programs(1) - 1)
    def _():
        width = n_heads * HEAD_DIM
        head_of_col = lax.broadcasted_iota(jnp.int32, (n_heads, width), 1) // HEAD_DIM
        own = head_of_col == lax.broadcasted_iota(jnp.int32, (n_heads, width), 0)
        for t in range(n_new):
            slab = jnp.where(own, acc_ref[t * n_heads:(t + 1) * n_heads, :], 0.0)
            ms = jnp.sum(slab * slab, axis=1, keepdims=True) * (1.0 / HEAD_DIM)
            normed = slab * lax.rsqrt(ms + NORM_EPS)
            o_ref[t:t + 1, :] = jnp.sum(normed, axis=0, keepdims=True) * gain_ref[...]


def _sb_decode(qbd, bias_col, tq_col, k_new, v_new, gain_row, cache_k, cache_v, page_table, n_new, n_heads):
    n_seq, n_rows, width = qbd.shape
    n_pool, page, _ = cache_k.shape
    n_pages = page_table.shape[1]
    pps = _tile(n_pages, 4, 1)
    n_steps = n_pages // pps

    def page_map(r):
        return lambda b, s, pt: (pt[b, n_pages - 1 - (s * pps + r)], 0, 0)

    kv_specs, kv_args = [], []
    for r in range(pps):
        kv_specs += [pl.BlockSpec((None, page, width), page_map(r))] * 2
        kv_args += [cache_k, cache_v]
    seq_map = lambda b, s, pt: (b, 0, 0)
    fix_map = lambda b, s, pt: (0, 0)
    grid_spec = pltpu.PrefetchScalarGridSpec(
        num_scalar_prefetch=1,
        grid=(n_seq, n_steps),
        in_specs=[
            pl.BlockSpec((None, n_rows, width), seq_map),
            pl.BlockSpec((n_rows, 1), fix_map),
            pl.BlockSpec((n_rows, 1), fix_map),
            pl.BlockSpec((None, page, width), seq_map),
            pl.BlockSpec((None, page, width), seq_map),
            pl.BlockSpec((1, width), fix_map),
        ] + kv_specs,
        out_specs=pl.BlockSpec((None, n_new, width), seq_map),
        scratch_shapes=[pltpu.VMEM((n_rows, 1), _F32), pltpu.VMEM((n_rows, width), _F32)],
    )
    return pl.pallas_call(
        functools.partial(_sb_decode_body, pages_per_step=pps, n_new=n_new, n_heads=n_heads),
        out_shape=jax.ShapeDtypeStruct((n_seq, n_new, width), _F32),
        grid_spec=grid_spec,
        compiler_params=_params(("arbitrary", "arbitrary")),
        name="sb_decode",
    )(page_table, qbd, bias_col, tq_col, k_new, v_new, gain_row, *kv_args)


def _gated_group_norm(o, gain, gate):
    mu = jnp.mean(o, axis=-1, keepdims=True)
    d = o - mu
    var = jnp.mean(d * d, axis=-1, keepdims=True)
    return d * lax.rsqrt(var + NORM_EPS) * gain * (gate * jax.nn.sigmoid(gate))


def _ret_prompt_body(logg_ref, q_ref, k_ref, v_ref, g_ref, gain_ref, o_ref, s_ref, state_ref, *, chunk):
    h = pl.program_id(1)
    c = pl.program_id(2)
    lg = logg_ref[h]

    @pl.when(c == 0)
    def _():
        state_ref[...] = jnp.zeros_like(state_ref)

    q = q_ref[...].astype(_BF16)
    k = k_ref[...]
    v = v_ref[...].astype(_BF16)
    li = lax.broadcasted_iota(jnp.int32, (chunk, chunk), 0)
    mi = lax.broadcasted_iota(jnp.int32, (chunk, chunk), 1)
    dist = (li - mi).astype(_F32)
    decay = jnp.where(dist >= 0, jnp.exp(jnp.maximum(dist, 0.0) * lg), 0.0)
    scores = lax.dot_general(q, k.astype(_BF16), _NT, preferred_element_type=_F32) * decay
    o = jnp.dot(scores.astype(_BF16), v, preferred_element_type=_F32)
    pos = lax.broadcasted_iota(jnp.int32, (chunk, 1), 0).astype(_F32)
    state = state_ref[...]
    o = o + jnp.dot(q, state.astype(_BF16), preferred_element_type=_F32) * jnp.exp((pos + 1.0) * lg)
    k_tail = k * jnp.exp((chunk - 1.0 - pos) * lg)
    full = jnp.full((1, HEAD_DIM), float(chunk), _F32)
    new_state = state * jnp.exp(full * lg) + jnp.dot(k_tail.T.astype(_BF16), v, preferred_element_type=_F32)
    state_ref[...] = new_state

    @pl.when(c == pl.num_programs(2) - 1)
    def _():
        s_ref[...] = new_state

    o_ref[...] = _gated_group_norm(o, gain_ref[...], g_ref[...]).astype(o_ref.dtype)


def _ret_prompt(proj, log_g, ret_gain, batch, seq, n_heads):
    chunk = _tile(seq, 256, 128)
    nc = seq // chunk
    col = lambda seg: (lambda b, h, c: (b * nc + c, seg * n_heads + h))
    return pl.pallas_call(
        functools.partial(_ret_prompt_body, chunk=chunk),
        out_shape=(jax.ShapeDtypeStruct((batch * seq, n_heads * HEAD_DIM), _BF16),
                   jax.ShapeDtypeStruct((batch, n_heads, HEAD_DIM, HEAD_DIM), _F32)),
        grid=(batch, n_heads, nc),
        in_specs=[
            pl.BlockSpec(memory_space=pltpu.SMEM),
            pl.BlockSpec((chunk, HEAD_DIM), col(3)),
            pl.BlockSpec((chunk, HEAD_DIM), col(4)),
            pl.BlockSpec((chunk, HEAD_DIM), col(5)),
            pl.BlockSpec((chunk, HEAD_DIM), col(6)),
            pl.BlockSpec((None, 1, HEAD_DIM), lambda b, h, c: (h, 0, 0)),
        ],
        out_specs=(pl.BlockSpec((chunk, HEAD_DIM), lambda b, h, c: (b * nc + c, h)),
                   pl.BlockSpec((None, None, HEAD_DIM, HEAD_DIM), lambda b, h, c: (b, h, 0, 0))),
        scratch_shapes=[pltpu.VMEM((HEAD_DIM, HEAD_DIM), _F32)],
        compiler_params=_params(("arbitrary", "arbitrary", "arbitrary")),
        name="ret_prompt",
    )(log_g, proj, proj, proj, proj, ret_gain.reshape(n_heads, 1, HEAD_DIM))


def _ret_sample_body(logg_ref, q_ref, k_ref, v_ref, g_ref, gain_ref, seqc_ref, seqr_ref, posc_ref, posr_ref,
                     s0_ref, o_ref, s_ref, *, n_new):
    h = pl.program_id(0)
    lg = logg_ref[h]
    n_seq = s0_ref.shape[0]
    q = q_ref[...].astype(_BF16)
    k = k_ref[...]
    v = v_ref[...].astype(_BF16)
    seq_c, seq_r = seqc_ref[...], seqr_ref[...]
    pos_c, pos_r = posc_ref[...], posr_ref[...]
    dist = pos_c - pos_r
    decay = jnp.where(seq_c == seq_r, jnp.where(dist >= 0, jnp.exp(jnp.maximum(dist, 0.0) * lg), 0.0), 0.0)
    scores = lax.dot_general(q, k.astype(_BF16), _NT, preferred_element_type=_F32) * decay
    o = jnp.dot(scores.astype(_BF16), v, preferred_element_type=_F32)
    q_decay = jnp.exp((pos_c + 1.0) * lg)
    k_tail_t = k.T * jnp.exp((n_new - 1.0 - pos_r) * lg)
    full = jnp.full((1, HEAD_DIM), float(n_new), _F32)
    state_decay = jnp.exp(full * lg)

    def body(b, o):
        state = s0_ref[b]
        from_state = jnp.dot(q, state.astype(_BF16), preferred_element_type=_F32) * q_decay
        o = o + jnp.where(seq_c == b, from_state, 0.0)
        kt = jnp.where(seq_r == b, k_tail_t, 0.0).astype(_BF16)
        s_ref[b] = state * state_decay + jnp.dot(kt, v, preferred_element_type=_F32)
        return o

    o = lax.fori_loop(0, n_seq, body, o)
    o_ref[...] = _gated_group_norm(o, gain_ref[...], g_ref[...]).astype(o_ref.dtype)


def _ret_sample(proj_s, log_g, ret_gain, state, n_seq, n_new, n_heads):
    ts = n_seq * n_new
    seq_id = jnp.repeat(jnp.arange(n_seq, dtype=jnp.int32), n_new)
    pos = jnp.tile(jnp.arange(n_new, dtype=_F32), n_seq)
    col = lambda seg: (lambda h: (0, seg * n_heads + h))
    fix = lambda h: (0, 0)
    return pl.pallas_call(
        functools.partial(_ret_sample_body, n_new=n_new),
        out_shape=(jax.ShapeDtypeStruct((ts, n_heads * HEAD_DIM), _BF16),
                   jax.ShapeDtypeStruct((n_seq, n_heads, HEAD_DIM, HEAD_DIM), _F32)),
        grid=(n_heads,),
        in_specs=[
            pl.BlockSpec(memory_space=pltpu.SMEM),
            pl.BlockSpec((ts, HEAD_DIM), col(3)),
            pl.BlockSpec((ts, HEAD_DIM), col(4)),
            pl.BlockSpec((ts, HEAD_DIM), col(5)),
            pl.BlockSpec((ts, HEAD_DIM), col(6)),
            pl.BlockSpec((None, 1, HEAD_DIM), lambda h: (h, 0, 0)),
            pl.BlockSpec((ts, 1), fix),
            pl.BlockSpec((1, ts), fix),
            pl.BlockSpec((ts, 1), fix),
            pl.BlockSpec((1, ts), fix),
            pl.BlockSpec((n_seq, None, HEAD_DIM, HEAD_DIM), lambda h: (0, h, 0, 0)),
        ],
        out_specs=(pl.BlockSpec((ts, HEAD_DIM), lambda h: (0, h)),
                   pl.BlockSpec((n_seq, None, HEAD_DIM, HEAD_DIM), lambda h: (0, h, 0, 0))),
        compiler_params=_params(("arbitrary",)),
        name="ret_sample",
    )(log_g, proj_s, proj_s, proj_s, proj_s, ret_gain.reshape(n_heads, 1, HEAD_DIM),
      seq_id.reshape(ts, 1), seq_id.reshape(1, ts), pos.reshape(ts, 1), pos.reshape(1, ts), state)


def _outproj_body(oa_ref, ob_ref, x_ref, w_ref, g_ref, hp_ref, xn_ref):
    half = oa_ref.shape[1]
    hp = (x_ref[...]
          + jnp.dot(oa_ref[...], w_ref[:half, :], preferred_element_type=_F32)
          + jnp.dot(ob_ref[...], w_ref[half:, :], preferred_element_type=_F32))
    hp_ref[...] = hp
    ms = jnp.mean(hp * hp, axis=-1, keepdims=True)
    xn_ref[...] = (hp * lax.rsqrt(ms + NORM_EPS) * g_ref[...]).astype(xn_ref.dtype)


def _outproj(o_a, o_b, x, w_bf16, gain2):
    T, D = x.shape
    half = o_a.shape[1]
    tm = _tile(T, 320, 16)
    return pl.pallas_call(
        _outproj_body,
        out_shape=(jax.ShapeDtypeStruct((T, D), _F32), jax.ShapeDtypeStruct((T, D), _BF16)),
        grid=(T // tm,),
        in_specs=[
            pl.BlockSpec((tm, half), lambda i: (i, 0)),
            pl.BlockSpec((tm, half), lambda i: (i, 0)),
            pl.BlockSpec((tm, D), lambda i: (i, 0)),
            pl.BlockSpec((2 * half, D), lambda i: (0, 0)),
            pl.BlockSpec((1, D), lambda i: (0, 0)),
        ],
        out_specs=(pl.BlockSpec((tm, D), lambda i: (i, 0)), pl.BlockSpec((tm, D), lambda i: (i, 0))),
        compiler_params=_params(("arbitrary",)),
        name="outproj",
    )(o_a, o_b, x, w_bf16, gain2)


def _top_values(s, n):
    vals = []
    cur = s
    for _ in range(n):
        m = jnp.max(cur, axis=0, keepdims=True)
        vals.append(m)
        cur = jnp.where(cur == m, _NEG_INF, cur)
    return vals


def _stack_rows(rows):
    n = len(rows)
    ridx = lax.broadcasted_iota(jnp.int32, (n, rows[0].shape[1]), 0)
    out = jnp.broadcast_to(rows[0], ridx.shape)
    for i in range(1, n):
        out = jnp.where(ridx == i, rows[i], out)
    return out


def _router_body(xn_ref, wq_ref, sk_ref, p1_ref, e1_ref, p2_ref, e2_ref, tau_ref, q_ref):
    k = PEER_TOPK
    q_ref[...] = jnp.dot(xn_ref[...], wq_ref[...], preferred_element_type=_F32)
    tt = xn_ref.shape[0]
    ridx = lax.broadcasted_iota(jnp.int32, (k, tt), 0)

    def head(h, _):
        def scores(c):
            start = pl.multiple_of((2 * h + c) * PEER_HALF, PEER_HALF)
            qc = q_ref[:, pl.ds(start, PEER_HALF)]
            sk = sk_ref[h, c]
            q_hi = qc.astype(_BF16)
            q_lo = (qc - q_hi.astype(_F32)).astype(_BF16)
            sk_hi = sk.astype(_BF16)
            sk_lo = (sk - sk_hi.astype(_F32)).astype(_BF16)
            dot = lambda a, b: lax.dot_general(a, b, _NT, preferred_element_type=_F32)
            return dot(sk_hi, q_hi) + (dot(sk_hi, q_lo) + dot(sk_lo, q_hi))

        s1, s2 = scores(0), scores(1)
        t1, t2 = _top_values(s1, k), _top_values(s2, k)
        t1s, t2s = _stack_rows(t1), _stack_rows(t2)
        cands = [jnp.where(ridx < k // (i + 1), t1[i] + t2s, _NEG_INF) for i in range(k // 2)]
        cands.append(jnp.where(ridx >= k // 2, t1s + t2[0], _NEG_INF))
        top = t1[0] + t2[0]
        z = jnp.zeros_like(top)
        tau = top
        for _ in range(k):
            m = cands[0]
            for cnd in cands[1:]:
                m = jnp.maximum(m, cnd)
            m = jnp.max(m, axis=0, keepdims=True)
            z = z + jnp.exp(m - top)
            tau = m
            cands = [jnp.where(cnd == m, _NEG_INF, cnd) for cnd in cands]
        p1_ref[h] = jnp.where(s1 >= t1[k - 1], s1, _NEG_INF)
        p2_ref[h] = jnp.where(s2 >= t2[k - 1], s2, _NEG_INF)
        e1_ref[h] = jnp.exp(s1 - t1[0])
        e2_ref[h] = jnp.exp(s2 - t2[0]) / z
        tau_ref[h] = tau
        return 0

    lax.fori_loop(0, PEER_HEADS, head, 0)


def _router(xn, wq_bf16, subkeys):
    T, D = xn.shape
    tt = _tile(T, 640, 128)
    big = jax.ShapeDtypeStruct((PEER_HEADS, PEER_KEYS, T), _F32)
    big_spec = pl.BlockSpec((PEER_HEADS, PEER_KEYS, tt), lambda i: (0, 0, i))
    return pl.pallas_call(
        _router_body,
        out_shape=(big, big, big, big, jax.ShapeDtypeStruct((PEER_HEADS, 1, T), _F32)),
        grid=(T // tt,),
        in_specs=[
            pl.BlockSpec((tt, D), lambda i: (i, 0)),
            pl.BlockSpec(wq_bf16.shape, lambda i: (0, 0)),
            pl.BlockSpec(subkeys.shape, lambda i: (0, 0, 0, 0)),
        ],
        out_specs=(big_spec, big_spec, big_spec, big_spec,
                   pl.BlockSpec((PEER_HEADS, 1, tt), lambda i: (0, 0, i))),
        scratch_shapes=[pltpu.VMEM((tt, wq_bf16.shape[1]), _F32)],
        compiler_params=_params(("arbitrary",)),
        name="peer_router",
    )(xn, wq_bf16, subkeys)


def _gelu_tanh(x):
    c = 0.7978845608028654
    return x * (0.5 * (1.0 + jnp.tanh(c * (x + 0.044715 * (x * x * x)))))


def _experts_body(xn_ref, u_ref, vt_ref, p1_ref, e1_ref, p2_ref, e2_ref, tau_ref, hp_ref, y_ref, acc_ref, *,
                  a_per_tile):
    j = pl.program_id(1)

    @pl.when(j == 0)
    def _():
        acc_ref[...] = jnp.zeros_like(acc_ref)

    hidden = lax.dot_general(u_ref[...], xn_ref[...], _NT, preferred_element_type=_F32)
    parts = []
    for aa in range(a_per_tile):
        a = j * a_per_tile + aa
        gate = jnp.zeros((PEER_KEYS, hidden.shape[1]), _F32)
        for h in range(PEER_HEADS):
            cand = p1_ref[h, pl.ds(a, 1), :] + p2_ref[h]
            gate = gate + jnp.where(cand >= tau_ref[h], e1_ref[h, pl.ds(a, 1), :] * e2_ref[h], 0.0)
        rows = hidden[aa * PEER_KEYS:(aa + 1) * PEER_KEYS, :]
        parts.append((_gelu_tanh(rows) * gate).astype(_BF16))
    act = jnp.concatenate(parts, axis=0)
    acc_ref[...] += jnp.dot(vt_ref[...], act, preferred_element_type=_F32)

    @pl.when(j == pl.num_programs(1) - 1)
    def _():
        y_ref[...] = hp_ref[...] + acc_ref[...].T


def _experts(xn, u_bf16, vt_bf16, p1, e1, p2, e2, tau, hp):
    T, D = xn.shape
    n_experts = u_bf16.shape[0]
    tt = _tile(T, 640, 128)
    a_per_tile = 4
    et = a_per_tile * PEER_KEYS
    once = pl.Buffered(1)
    tok = pl.BlockSpec((PEER_HEADS, PEER_KEYS, tt), lambda i, j: (0, 0, i), pipeline_mode=once)
    return pl.pallas_call(
        functools.partial(_experts_body, a_per_tile=a_per_tile),
        out_shape=jax.ShapeDtypeStruct((T, D), _F32),
        grid=(T // tt, n_experts // et),
        in_specs=[
            pl.BlockSpec((tt, D), lambda i, j: (i, 0), pipeline_mode=once),
            pl.BlockSpec((et, D), lambda i, j: (j, 0)),
            pl.BlockSpec((D, et), lambda i, j: (0, j)),
            tok, tok, tok, tok,
            pl.BlockSpec((PEER_HEADS, 1, tt), lambda i, j: (0, 0, i)),
            pl.BlockSpec((tt, D), lambda i, j: (i, 0), pipeline_mode=once),
        ],
        out_specs=pl.BlockSpec((tt, D), lambda i, j: (i, 0)),
        scratch_shapes=[pltpu.VMEM((D, tt), _F32)],
        compiler_params=_params(("arbitrary", "arbitrary")),
        name="peer_experts",
    )(xn, u_bf16, vt_bf16, p1, e1, p2, e2, tau, hp)


def _rope_tables(pos):
    half = HEAD_DIM // 2
    freqs = ROPE_BASE ** (-jnp.arange(half, dtype=_F32) / half)
    ang = pos.astype(_F32)[:, None] * freqs[None, :]
    cos, sin = jnp.cos(ang), jnp.sin(ang)
    return jnp.concatenate([cos, cos], axis=-1), jnp.concatenate([-sin, sin], axis=-1)


def kernel(x_prompt, x_sample, cache_k, cache_v, state_ret, page_table, norm1_gain, w_in, q_norm_gain,
           k_norm_gain, sb_bias, sb_out_gain, ret_out_gain, w_out, norm2_gain, peer_w_q, peer_subkeys,
           peer_u, peer_v):
    batch, seq, d_model = x_prompt.shape
    n_seq, n_new, _ = x_sample.shape
    depth, n_pool, page = cache_k.shape[:3]
    n_heads = cache_k.shape[3]
    width = n_heads * HEAD_DIM
    assert w_in.shape[2] == 7 * width and w_out.shape[1] == 2 * width
    assert n_heads * n_new % 8 == 0 and seq % (n_seq * n_new) == 0
    tp, ts = batch * seq, n_seq * n_new
    past = page_table.shape[1] * page

    pos = jnp.concatenate([jnp.tile(jnp.arange(seq), batch), jnp.tile(past + jnp.arange(n_new), n_seq)])
    cos_t, sin_t = _rope_tables(pos)
    log_g = jnp.log1p(-jnp.power(2.0, -5.0 - jnp.arange(n_heads, dtype=_F32)))
    tq_col = jnp.repeat(jnp.arange(n_new, dtype=jnp.int32), n_heads).reshape(n_new * n_heads, 1)
    eye = jnp.eye(n_heads, dtype=_F32)

    h_all = jnp.concatenate([x_prompt.reshape(tp, d_model), x_sample.reshape(ts, d_model)], axis=0)
    outs = [[] for _ in range(6)]
    for l in range(depth):
        proj = _inproj(h_all, norm1_gain[l].reshape(1, d_model), w_in[l].astype(_BF16), cos_t, sin_t,
                       q_norm_gain[l], k_norm_gain[l], n_heads)
        proj_s = proj[tp:]
        oa_p = _sb_prompt(proj, sb_bias[l], sb_out_gain[l], batch, seq, n_heads)
        ob_p, ret_p = _ret_prompt(proj, log_g, ret_out_gain[l], batch, seq, n_heads)
        q_s = proj_s[:, :width].reshape(n_seq, n_new, n_heads, HEAD_DIM)
        qbd = jnp.einsum("blhd,hg->blhgd", q_s, eye).reshape(n_seq, n_new * n_heads, width).astype(_BF16)
        pad = lambda t: jnp.pad(t.reshape(n_seq, n_new, width), ((0, 0), (0, page - n_new), (0, 0)))
        bias_col = jnp.tile(sb_bias[l], n_new).reshape(n_new * n_heads, 1)
        oa_s = _sb_decode(qbd, bias_col, tq_col, pad(proj_s[:, width:2 * width]),
                          pad(proj_s[:, 2 * width:3 * width]), sb_out_gain[l].reshape(1, width),
                          cache_k[l].reshape(n_pool, page, width), cache_v[l].reshape(n_pool, page, width),
                          page_table, n_new, n_heads)
        ob_s, ret_s = _ret_sample(proj_s, log_g, ret_out_gain[l], state_ret[l], n_seq, n_new, n_heads)
        o_a = jnp.concatenate([oa_p, oa_s.reshape(ts, width).astype(_BF16)], axis=0)
        o_b = jnp.concatenate([ob_p, ob_s], axis=0)
        hp, xn2 = _outproj(o_a, o_b, h_all, w_out[l].astype(_BF16), norm2_gain[l].reshape(1, d_model))
        p1, e1, p2, e2, tau = _router(xn2, peer_w_q[l].astype(_BF16), peer_subkeys[l])
        h_all = _experts(xn2, peer_u[l].astype(_BF16), peer_v[l].T.astype(_BF16), p1, e1, p2, e2, tau, hp)

        k_all, v_all = proj[:, width:2 * width], proj[:, 2 * width:3 * width]
        outs[0].append(k_all[:tp].reshape(batch, seq, n_heads, HEAD_DIM))
        outs[1].append(v_all[:tp].reshape(batch, seq, n_heads, HEAD_DIM))
        outs[2].append(ret_p)
        outs[3].append(k_all[tp:].reshape(n_seq, n_new, n_heads, HEAD_DIM))
        outs[4].append(v_all[tp:].reshape(n_seq, n_new, n_heads, HEAD_DIM))
        outs[5].append(ret_s)
    return (h_all[:tp].reshape(batch, seq, d_model), h_all[tp:].reshape(n_seq, n_new, d_model),
            *(jnp.stack(o) for o in outs))
```

```python
import functools

import jax
import jax.numpy as jnp
from jax import lax
from jax.experimental import pallas as pl
from jax.experimental.pallas import tpu as pltpu

HEAD_DIM = 128
NORM_EPS = 1e-6
ROPE_BASE = 10000.0
PEER_HEADS = 8
PEER_KEYS = 128
PEER_TOPK = 16
PEER_HALF = 128
QK_SCALE = HEAD_DIM ** -0.5

_F32 = jnp.float32
_BF16 = jnp.bfloat16
_NEG_INF = float("-inf")
_V7X_VMEM_LIMIT_BYTES = 56 * 1024 * 1024

_NT = (((1,), (1,)), ((), ()))


def _params(sem):
    return pltpu.CompilerParams(dimension_semantics=sem, vmem_limit_bytes=_V7X_VMEM_LIMIT_BYTES)


def _tile(n, cap, mult):
    best = None
    for d in range(mult, min(n, cap) + 1, mult):
        if n % d == 0:
            best = d
    assert best is not None, (n, cap, mult)
    return best


def _softplus(z):
    return jnp.maximum(z, 0.0) + jnp.log(1.0 + jnp.exp(-jnp.abs(z)))


def _strict_lower_ones(n):
    row = lax.broadcasted_iota(jnp.int32, (n, n), 0)
    col = lax.broadcasted_iota(jnp.int32, (n, n), 1)
    return jnp.where(row > col, 1.0, 0.0).astype(_BF16)


def _inproj_body(x_ref, g_ref, w_ref, cos_ref, sin_ref, qg_ref, kg_ref, o_ref, xn_ref, *, n_heads):
    j = pl.program_id(1)

    @pl.when(j == 0)
    def _():
        x = x_ref[...]
        ms = jnp.mean(x * x, axis=-1, keepdims=True)
        xn_ref[...] = (x * lax.rsqrt(ms + NORM_EPS) * g_ref[...]).astype(_BF16)

    acc = jnp.dot(xn_ref[...], w_ref[...], preferred_element_type=_F32)

    def per_head(fn):
        for h in range(n_heads):
            sl = slice(h * HEAD_DIM, (h + 1) * HEAD_DIM)
            o_ref[:, sl] = fn(acc[:, sl], h)

    def rms(a, gain):
        return a * lax.rsqrt(jnp.mean(a * a, axis=-1, keepdims=True) + NORM_EPS) * gain

    def rope(a):
        return a * cos_ref[...] + pltpu.roll(a, HEAD_DIM // 2, 1) * sin_ref[...]

    @pl.when(j == 0)
    def _():
        per_head(lambda a, h: rms(a, qg_ref[h:h + 1, :]) * QK_SCALE)

    @pl.when(j == 1)
    def _():
        per_head(lambda a, h: rms(a, kg_ref[h:h + 1, :]))

    @pl.when(j == 3)
    def _():
        per_head(lambda a, h: rope(a))

    @pl.when(j == 4)
    def _():
        per_head(lambda a, h: rope(a) * QK_SCALE)

    @pl.when((j == 2) | (j == 5) | (j == 6))
    def _():
        o_ref[...] = acc


def _inproj(x, gain, w_bf16, cos_t, sin_t, q_gain, k_gain, n_heads):
    T, D = x.shape
    width = n_heads * HEAD_DIM
    n_seg = w_bf16.shape[1] // width
    tm = _tile(T, 640, 8)
    return pl.pallas_call(
        functools.partial(_inproj_body, n_heads=n_heads),
        out_shape=jax.ShapeDtypeStruct((T, n_seg * width), _F32),
        grid=(T // tm, n_seg),
        in_specs=[
            pl.BlockSpec((tm, D), lambda i, j: (i, 0)),
            pl.BlockSpec((1, D), lambda i, j: (0, 0)),
            pl.BlockSpec((D, width), lambda i, j: (0, j)),
            pl.BlockSpec((tm, HEAD_DIM), lambda i, j: (i, 0)),
            pl.BlockSpec((tm, HEAD_DIM), lambda i, j: (i, 0)),
            pl.BlockSpec((n_heads, HEAD_DIM), lambda i, j: (0, 0)),
            pl.BlockSpec((n_heads, HEAD_DIM), lambda i, j: (0, 0)),
        ],
        out_specs=pl.BlockSpec((tm, width), lambda i, j: (i, j)),
        scratch_shapes=[pltpu.VMEM((tm, D), _BF16)],
        compiler_params=_params(("arbitrary", "arbitrary")),
        name="inproj",
    )(x, gain, w_bf16, cos_t, sin_t, q_gain, k_gain)


def _sb_blocks(chains, bias, tri):
    masked = lambda x, mask: x if mask is None else jnp.where(mask, x, 0.0)
    zs = [lax.dot_general(q, ks, _NT, preferred_element_type=_F32) + bias for q, ks, _, _, _, _ in chains]
    sps = [masked(_softplus(z), c[5]) for z, c in zip(zs, chains)]
    laters = [jnp.dot(sp.astype(_BF16), tri, preferred_element_type=_F32) for sp in sps]
    out = []
    for z, sp, later, (_, _, vs, carry, acc, mask) in zip(zs, sps, laters, chains):
        w = masked(jnp.exp(z - sp - (carry + later)), mask)
        acc = acc + jnp.dot(w.astype(_BF16), vs, preferred_element_type=_F32)
        out.append((carry + jnp.sum(sp, axis=1, keepdims=True), acc))
    return out


def _sb_prompt_body(bias_ref, q_ref, k_ref, v_ref, gain_ref, o_ref, *, blk):
    h = pl.program_id(1)
    i = pl.program_id(2)
    bias = bias_ref[h]
    q_lo = q_ref[:blk, :].astype(_BF16)
    q_hi = q_ref[blk:, :].astype(_BF16)
    tri = _strict_lower_ones(blk)
    row = lax.broadcasted_iota(jnp.int32, (blk, blk), 0)
    col = lax.broadcasted_iota(jnp.int32, (blk, blk), 1)
    causal = col < row

    def kv(kb):
        start = pl.multiple_of(kb * blk, blk)
        return k_ref[pl.ds(start, blk), :].astype(_BF16), v_ref[pl.ds(start, blk), :].astype(_BF16)

    zero = (jnp.zeros((blk, 1), _F32), jnp.zeros((blk, HEAD_DIM), _F32))
    k1, v1 = kv(2 * i + 1)
    k0, v0 = kv(2 * i)
    lo, hi = _sb_blocks([(q_lo, k0, v0, *zero, causal), (q_hi, k1, v1, *zero, causal)], bias, tri)
    (hi,) = _sb_blocks([(q_hi, k0, v0, *hi, None)], bias, tri)

    def body(s, c):
        ks, vs = kv(2 * i - 1 - s)
        return tuple(_sb_blocks([(q_lo, ks, vs, *c[0], None), (q_hi, ks, vs, *c[1], None)], bias, tri))

    lo, hi = lax.fori_loop(0, 2 * i, body, (lo, hi))
    for part, (_, acc) in enumerate((lo, hi)):
        ms = jnp.mean(acc * acc, axis=-1, keepdims=True)
        o_ref[part * blk:(part + 1) * blk, :] = (acc * lax.rsqrt(ms + NORM_EPS) * gain_ref[...]).astype(o_ref.dtype)


def _sb_prompt(proj, sb_bias, sb_gain, batch, seq, n_heads):
    blk = _tile(seq // 2, 256, 128)
    nq = seq // (2 * blk)
    return pl.pallas_call(
        functools.partial(_sb_prompt_body, blk=blk),
        out_shape=jax.ShapeDtypeStruct((batch * seq, n_heads * HEAD_DIM), _BF16),
        grid=(batch, n_heads, nq),
        in_specs=[
            pl.BlockSpec(memory_space=pltpu.SMEM),
            pl.BlockSpec((2 * blk, HEAD_DIM), lambda b, h, i: (b * nq + i, h)),
            pl.BlockSpec((seq, HEAD_DIM), lambda b, h, i: (b, n_heads + h)),
            pl.BlockSpec((seq, HEAD_DIM), lambda b, h, i: (b, 2 * n_heads + h)),
            pl.BlockSpec((None, 1, HEAD_DIM), lambda b, h, i: (h, 0, 0)),
        ],
        out_specs=pl.BlockSpec((2 * blk, HEAD_DIM), lambda b, h, i: (b * nq + i, h)),
        compiler_params=_params(("arbitrary", "arbitrary", "arbitrary")),
        name="sb_prompt",
    )(sb_bias, proj, proj, proj, sb_gain.reshape(n_heads, 1, HEAD_DIM))


_LANES = 128
_MXU_COLS = 256


def _sb_flat_blocks(q, kfs, vfs, bias, tri_ones, carry, acc, valid):
    rows, groups = q.shape[0], kfs[0].shape[0] // _LANES
    zs = [lax.dot_general(q, kf, _NT, preferred_element_type=_F32) + bias for kf in kfs]
    sps = [jnp.where(valid, _softplus(z), 0.0) for z in zs]
    boths = []
    for sp in sps:
        parts = [sp[:, g * _LANES:(g + 1) * _LANES] for g in range(groups)]
        stacked = parts[0] if groups == 1 else jnp.concatenate(parts, axis=0)
        boths.append(jnp.dot(stacked.astype(_BF16), tri_ones, preferred_element_type=_F32))
    ws = []
    for z, sp, both in zip(zs, sps, boths):
        later = [None] * groups
        for g in reversed(range(groups)):
            later[g] = both[g * rows:(g + 1) * rows, :_LANES] + carry
            carry = carry + both[g * rows:(g + 1) * rows, _LANES:]
        later = later[0] if groups == 1 else jnp.concatenate(later, axis=1)
        ws.append(jnp.where(valid, jnp.exp(z - sp - later), 0.0).astype(_BF16))
    for w, vf in zip(ws, vfs):
        acc = acc + jnp.dot(w, vf, preferred_element_type=_F32)
    return carry, acc


def _sb_decode_body(pt_ref, q_ref, bias_ref, tq_ref, hq_ref, hk_ref, hkn_ref, keyn_ref, knew_ref, vnew_ref,
                    gain_ref, *rest, pages_per_step):
    del pt_ref
    kv_refs = rest[:2 * pages_per_step]
    o_ref, carry_ref, acc_ref = rest[2 * pages_per_step:]
    s = pl.program_id(1)
    q = q_ref[...].astype(_BF16)
    bias = bias_ref[...]
    row = lax.broadcasted_iota(jnp.int32, (_LANES, 2 * _LANES), 0)
    col = lax.broadcasted_iota(jnp.int32, (_LANES, 2 * _LANES), 1)
    tri_ones = jnp.where(col >= _LANES, 1.0, jnp.where(row > col, 1.0, 0.0)).astype(_BF16)
    own = hq_ref[...] == hk_ref[...]

    @pl.when(s == 0)
    def _():
        never = jnp.iinfo(jnp.int32).max
        valid = jnp.where(hq_ref[...] == hkn_ref[...], keyn_ref[...], never) < tq_ref[...]
        carry, acc = _sb_flat_blocks(q, [knew_ref[...].astype(_BF16)], [vnew_ref[...].astype(_BF16)], bias,
                                     tri_ones, jnp.zeros(carry_ref.shape, _F32), jnp.zeros(acc_ref.shape, _F32),
                                     valid)
        carry_ref[...] = carry
        acc_ref[...] = acc

    def flat(ref):
        page = ref[...]
        return page.reshape(page.shape[0] * page.shape[1], page.shape[2]).astype(_BF16)

    carry, acc = _sb_flat_blocks(q, [flat(r) for r in kv_refs[0::2]], [flat(r) for r in kv_refs[1::2]], bias,
                                 tri_ones, carry_ref[...], acc_ref[...], own)
    carry_ref[...] = carry
    acc_ref[...] = acc

    @pl.when(s == pl.num_programs(1) - 1)
    def _():
        ms = jnp.mean(acc * acc, axis=-1, keepdims=True)
        o_ref[...] = acc * lax.rsqrt(ms + NORM_EPS) * gain_ref[...]


def _sb_decode(q_rows, k_new, v_new, sb_bias, sb_gain, cache_k, cache_v, layer, page_table, n_new):
    n_seq, n_rows, hd = q_rows.shape
    page, n_heads = cache_k.shape[2], cache_k.shape[3]
    n_pages = page_table.shape[1]
    pps = _tile(n_pages, 8, 1)
    n_steps = n_pages // pps
    assert n_rows <= _LANES and _LANES % n_heads == 0

    i32 = jnp.int32
    col = lambda v: v.reshape(n_rows, 1)
    bias_col = col(jnp.tile(sb_bias, n_new))
    tq_col = col(jnp.repeat(jnp.arange(n_new, dtype=i32), n_heads))
    hq_col = col(jnp.tile(jnp.arange(n_heads, dtype=i32), n_new))
    hk_row = jnp.tile(jnp.arange(n_heads, dtype=i32), page).reshape(1, page * n_heads)
    hkn_row = jnp.tile(jnp.arange(n_heads, dtype=i32), _LANES // n_heads).reshape(1, _LANES)
    keyn_row = jnp.repeat(jnp.arange(_LANES // n_heads, dtype=i32), n_heads).reshape(1, _LANES)
    gain_rows = jnp.tile(sb_gain, (n_new, 1))
    pad = lambda t: jnp.pad(t, ((0, 0), (0, _LANES - n_rows), (0, 0)))

    def page_map(r):
        return lambda b, s, pt: (layer, pt[b, n_pages - 1 - (s * pps + r)], 0, 0, 0)

    kv_specs, kv_args = [], []
    for r in range(pps):
        kv_specs += [pl.BlockSpec((None, None, page, n_heads, hd), page_map(r))] * 2
        kv_args += [cache_k, cache_v]
    seq_map = lambda b, s, pt: (b, 0, 0)
    fix_map = lambda b, s, pt: (0, 0)
    fixed = lambda a: pl.BlockSpec(a.shape, fix_map)
    small = [bias_col, tq_col, hq_col, hk_row, hkn_row, keyn_row]
    grid_spec = pltpu.PrefetchScalarGridSpec(
        num_scalar_prefetch=1,
        grid=(n_seq, n_steps),
        in_specs=[pl.BlockSpec((None, n_rows, hd), seq_map)] + [fixed(a) for a in small] + [
            pl.BlockSpec((None, _LANES, hd), seq_map),
            pl.BlockSpec((None, _LANES, hd), seq_map),
            fixed(gain_rows),
        ] + kv_specs,
        out_specs=pl.BlockSpec((None, n_rows, hd), seq_map),
        scratch_shapes=[pltpu.VMEM((n_rows, _LANES), _F32), pltpu.VMEM((n_rows, hd), _F32)],
    )
    return pl.pallas_call(
        functools.partial(_sb_decode_body, pages_per_step=pps),
        out_shape=jax.ShapeDtypeStruct((n_seq, n_rows, hd), _F32),
        grid_spec=grid_spec,
        compiler_params=_params(("arbitrary", "arbitrary")),
        name="sb_decode",
    )(page_table, q_rows, *small, pad(k_new), pad(v_new), gain_rows, *kv_args)


def _gated_group_norm(o, gain, gate):
    mu = jnp.mean(o, axis=-1, keepdims=True)
    d = o - mu
    var = jnp.mean(d * d, axis=-1, keepdims=True)
    return d * lax.rsqrt(var + NORM_EPS) * gain * (gate * jax.nn.sigmoid(gate))


def _ret_prompt_body(logg_ref, q_ref, k_ref, v_ref, g_ref, gain_ref, o_ref, s_ref, state_ref, *, chunk):
    h = pl.program_id(1)
    c = pl.program_id(2)
    lg = logg_ref[h]

    @pl.when(c == 0)
    def _():
        state_ref[...] = jnp.zeros_like(state_ref)

    q = q_ref[...].astype(_BF16)
    k = k_ref[...]
    v = v_ref[...].astype(_BF16)
    li = lax.broadcasted_iota(jnp.int32, (chunk, chunk), 0)
    mi = lax.broadcasted_iota(jnp.int32, (chunk, chunk), 1)
    dist = (li - mi).astype(_F32)
    decay = jnp.where(dist >= 0, jnp.exp(jnp.maximum(dist, 0.0) * lg), 0.0)
    scores = lax.dot_general(q, k.astype(_BF16), _NT, preferred_element_type=_F32) * decay
    o = jnp.dot(scores.astype(_BF16), v, preferred_element_type=_F32)
    pos = lax.broadcasted_iota(jnp.int32, (chunk, 1), 0).astype(_F32)
    state = state_ref[...]
    o = o + jnp.dot(q, state.astype(_BF16), preferred_element_type=_F32) * jnp.exp((pos + 1.0) * lg)
    k_tail = k * jnp.exp((chunk - 1.0 - pos) * lg)
    full = jnp.full((1, HEAD_DIM), float(chunk), _F32)
    new_state = state * jnp.exp(full * lg) + jnp.dot(k_tail.T.astype(_BF16), v, preferred_element_type=_F32)
    state_ref[...] = new_state

    @pl.when(c == pl.num_programs(2) - 1)
    def _():
        s_ref[...] = new_state

    o_ref[...] = _gated_group_norm(o, gain_ref[...], g_ref[...]).astype(o_ref.dtype)


def _ret_prompt(proj, log_g, ret_gain, batch, seq, n_heads):
    chunk = _tile(seq, 256, 128)
    nc = seq // chunk
    col = lambda seg: (lambda b, h, c: (b * nc + c, seg * n_heads + h))
    return pl.pallas_call(
        functools.partial(_ret_prompt_body, chunk=chunk),
        out_shape=(jax.ShapeDtypeStruct((batch * seq, n_heads * HEAD_DIM), _BF16),
                   jax.ShapeDtypeStruct((batch, n_heads, HEAD_DIM, HEAD_DIM), _F32)),
        grid=(batch, n_heads, nc),
        in_specs=[
            pl.BlockSpec(memory_space=pltpu.SMEM),
            pl.BlockSpec((chunk, HEAD_DIM), col(3)),
            pl.BlockSpec((chunk, HEAD_DIM), col(4)),
            pl.BlockSpec((chunk, HEAD_DIM), col(5)),
            pl.BlockSpec((chunk, HEAD_DIM), col(6)),
            pl.BlockSpec((None, 1, HEAD_DIM), lambda b, h, c: (h, 0, 0)),
        ],
        out_specs=(pl.BlockSpec((chunk, HEAD_DIM), lambda b, h, c: (b * nc + c, h)),
                   pl.BlockSpec((None, None, HEAD_DIM, HEAD_DIM), lambda b, h, c: (b, h, 0, 0))),
        scratch_shapes=[pltpu.VMEM((HEAD_DIM, HEAD_DIM), _F32)],
        compiler_params=_params(("arbitrary", "arbitrary", "arbitrary")),
        name="ret_prompt",
    )(log_g, proj, proj, proj, proj, ret_gain.reshape(n_heads, 1, HEAD_DIM))


def _ret_sample_body(logg_ref, q_ref, k_ref, v_ref, g_ref, gain_ref, seqc_ref, seqr_ref, posc_ref, posr_ref,
                     s0_ref, o_ref, s_ref, *, n_new):
    h = pl.program_id(0)
    lg = logg_ref[h]
    n_seq = s0_ref.shape[0]
    q = q_ref[...].astype(_BF16)
    k = k_ref[...]
    v = v_ref[...].astype(_BF16)
    seq_c, seq_r = seqc_ref[...], seqr_ref[...]
    pos_c, pos_r = posc_ref[...], posr_ref[...]
    dist = pos_c - pos_r
    decay = jnp.where(seq_c == seq_r, jnp.where(dist >= 0, jnp.exp(jnp.maximum(dist, 0.0) * lg), 0.0), 0.0)
    scores = lax.dot_general(q, k.astype(_BF16), _NT, preferred_element_type=_F32) * decay
    o = jnp.dot(scores.astype(_BF16), v, preferred_element_type=_F32)
    q_decay = jnp.exp((pos_c + 1.0) * lg)
    k_tail_t = k.T * jnp.exp((n_new - 1.0 - pos_r) * lg)
    full = jnp.full((1, HEAD_DIM), float(n_new), _F32)
    state_decay = jnp.exp(full * lg)

    def body(b, o):
        state = s0_ref[b]
        from_state = jnp.dot(q, state.astype(_BF16), preferred_element_type=_F32) * q_decay
        o = o + jnp.where(seq_c == b, from_state, 0.0)
        kt = jnp.where(seq_r == b, k_tail_t, 0.0).astype(_BF16)
        s_ref[b] = state * state_decay + jnp.dot(kt, v, preferred_element_type=_F32)
        return o

    o = lax.fori_loop(0, n_seq, body, o)
    o_ref[...] = _gated_group_norm(o, gain_ref[...], g_ref[...]).astype(o_ref.dtype)


def _ret_sample(proj_s, log_g, ret_gain, state, n_seq, n_new, n_heads):
    ts = n_seq * n_new
    seq_id = jnp.repeat(jnp.arange(n_seq, dtype=jnp.int32), n_new)
    pos = jnp.tile(jnp.arange(n_new, dtype=_F32), n_seq)
    col = lambda seg: (lambda h: (0, seg * n_heads + h))
    fix = lambda h: (0, 0)
    return pl.pallas_call(
        functools.partial(_ret_sample_body, n_new=n_new),
        out_shape=(jax.ShapeDtypeStruct((ts, n_heads * HEAD_DIM), _BF16),
                   jax.ShapeDtypeStruct((n_seq, n_heads, HEAD_DIM, HEAD_DIM), _F32)),
        grid=(n_heads,),
        in_specs=[
            pl.BlockSpec(memory_space=pltpu.SMEM),
            pl.BlockSpec((ts, HEAD_DIM), col(3)),
            pl.BlockSpec((ts, HEAD_DIM), col(4)),
            pl.BlockSpec((ts, HEAD_DIM), col(5)),
            pl.BlockSpec((ts, HEAD_DIM), col(6)),
            pl.BlockSpec((None, 1, HEAD_DIM), lambda h: (h, 0, 0)),
            pl.BlockSpec((ts, 1), fix),
            pl.BlockSpec((1, ts), fix),
            pl.BlockSpec((ts, 1), fix),
            pl.BlockSpec((1, ts), fix),
            pl.BlockSpec((n_seq, None, HEAD_DIM, HEAD_DIM), lambda h: (0, h, 0, 0)),
        ],
        out_specs=(pl.BlockSpec((ts, HEAD_DIM), lambda h: (0, h)),
                   pl.BlockSpec((n_seq, None, HEAD_DIM, HEAD_DIM), lambda h: (0, h, 0, 0))),
        compiler_params=_params(("arbitrary",)),
        name="ret_sample",
    )(log_g, proj_s, proj_s, proj_s, proj_s, ret_gain.reshape(n_heads, 1, HEAD_DIM),
      seq_id.reshape(ts, 1), seq_id.reshape(1, ts), pos.reshape(ts, 1), pos.reshape(1, ts), state)


def _outproj_body(oa_ref, ob_ref, x_ref, w_ref, g_ref, hp_ref, xn_ref):
    half = oa_ref.shape[1]
    hp = (x_ref[...]
          + jnp.dot(oa_ref[...], w_ref[:half, :], preferred_element_type=_F32)
          + jnp.dot(ob_ref[...], w_ref[half:, :], preferred_element_type=_F32))
    hp_ref[...] = hp
    ms = jnp.mean(hp * hp, axis=-1, keepdims=True)
    xn_ref[...] = (hp * lax.rsqrt(ms + NORM_EPS) * g_ref[...]).astype(xn_ref.dtype)


def _outproj(o_a, o_b, x, w_bf16, gain2):
    T, D = x.shape
    half = o_a.shape[1]
    tm = _tile(T, 320, 16)
    return pl.pallas_call(
        _outproj_body,
        out_shape=(jax.ShapeDtypeStruct((T, D), _F32), jax.ShapeDtypeStruct((T, D), _BF16)),
        grid=(T // tm,),
        in_specs=[
            pl.BlockSpec((tm, half), lambda i: (i, 0)),
            pl.BlockSpec((tm, half), lambda i: (i, 0)),
            pl.BlockSpec((tm, D), lambda i: (i, 0)),
            pl.BlockSpec((2 * half, D), lambda i: (0, 0)),
            pl.BlockSpec((1, D), lambda i: (0, 0)),
        ],
        out_specs=(pl.BlockSpec((tm, D), lambda i: (i, 0)), pl.BlockSpec((tm, D), lambda i: (i, 0))),
        compiler_params=_params(("arbitrary",)),
        name="outproj",
    )(o_a, o_b, x, w_bf16, gain2)


def _top_values(s, n):
    vals = []
    cur = s
    for _ in range(n):
        m = jnp.max(cur, axis=0, keepdims=True)
        vals.append(m)
        cur = jnp.where(cur == m, _NEG_INF, cur)
    return vals


def _stack_rows(rows):
    n = len(rows)
    ridx = lax.broadcasted_iota(jnp.int32, (n, rows[0].shape[1]), 0)
    out = jnp.broadcast_to(rows[0], ridx.shape)
    for i in range(1, n):
        out = jnp.where(ridx == i, rows[i], out)
    return out


def _router_body(xn_ref, wq_ref, sk_ref, p1_ref, e1_ref, p2_ref, e2_ref, tau_ref, q_ref):
    k = PEER_TOPK
    q_ref[...] = jnp.dot(xn_ref[...], wq_ref[...], preferred_element_type=_F32)
    tt = xn_ref.shape[0]
    ridx = lax.broadcasted_iota(jnp.int32, (k, tt), 0)

    def head(h, _):
        def scores(c):
            start = pl.multiple_of((2 * h + c) * PEER_HALF, PEER_HALF)
            qc = q_ref[:, pl.ds(start, PEER_HALF)]
            sk = sk_ref[h, c]
            q_hi = qc.astype(_BF16)
            q_lo = (qc - q_hi.astype(_F32)).astype(_BF16)
            sk_hi = sk.astype(_BF16)
            sk_lo = (sk - sk_hi.astype(_F32)).astype(_BF16)
            dot = lambda a, b: lax.dot_general(a, b, _NT, preferred_element_type=_F32)
            return dot(sk_hi, q_hi) + (dot(sk_hi, q_lo) + dot(sk_lo, q_hi))

        s1, s2 = scores(0), scores(1)
        t1, t2 = _top_values(s1, k), _top_values(s2, k)
        t1s, t2s = _stack_rows(t1), _stack_rows(t2)
        cands = [jnp.where(ridx < k // (i + 1), t1[i] + t2s, _NEG_INF) for i in range(k // 2)]
        cands.append(jnp.where(ridx >= k // 2, t1s + t2[0], _NEG_INF))
        top = t1[0] + t2[0]
        z = jnp.zeros_like(top)
        tau = top
        for _ in range(k):
            m = cands[0]
            for cnd in cands[1:]:
                m = jnp.maximum(m, cnd)
            m = jnp.max(m, axis=0, keepdims=True)
            z = z + jnp.exp(m - top)
            tau = m
            cands = [jnp.where(cnd == m, _NEG_INF, cnd) for cnd in cands]
        p1_ref[h] = jnp.where(s1 >= t1[k - 1], s1, _NEG_INF)
        p2_ref[h] = jnp.where(s2 >= t2[k - 1], s2, _NEG_INF)
        e1_ref[h] = jnp.exp(s1 - t1[0])
        e2_ref[h] = jnp.exp(s2 - t2[0]) / z
        tau_ref[h] = tau
        return 0

    lax.fori_loop(0, PEER_HEADS, head, 0)


def _router(xn, wq_bf16, subkeys):
    T, D = xn.shape
    tt = _tile(T, 640, 128)
    big = jax.ShapeDtypeStruct((PEER_HEADS, PEER_KEYS, T), _F32)
    big_spec = pl.BlockSpec((PEER_HEADS, PEER_KEYS, tt), lambda i: (0, 0, i))
    return pl.pallas_call(
        _router_body,
        out_shape=(big, big, big, big, jax.ShapeDtypeStruct((PEER_HEADS, 1, T), _F32)),
        grid=(T // tt,),
        in_specs=[
            pl.BlockSpec((tt, D), lambda i: (i, 0)),
            pl.BlockSpec(wq_bf16.shape, lambda i: (0, 0)),
            pl.BlockSpec(subkeys.shape, lambda i: (0, 0, 0, 0)),
        ],
        out_specs=(big_spec, big_spec, big_spec, big_spec,
                   pl.BlockSpec((PEER_HEADS, 1, tt), lambda i: (0, 0, i))),
        scratch_shapes=[pltpu.VMEM((tt, wq_bf16.shape[1]), _F32)],
        compiler_params=_params(("arbitrary",)),
        name="peer_router",
    )(xn, wq_bf16, subkeys)


def _gelu_tanh(x):
    c = 0.7978845608028654
    return x * (0.5 * (1.0 + jnp.tanh(c * (x + 0.044715 * (x * x * x)))))


def _experts_body(xn_ref, u_ref, v_ref, p1_ref, e1_ref, p2_ref, e2_ref, tau_ref, hp_ref, y_ref, act_ref, gate_ref, *,
                  a_per_tile):
    j = pl.program_id(1)
    last_tile = pl.num_programs(1) - 2

    cur = j % 2

    @pl.when(j == 0)
    def _():
        y_ref[...] = hp_ref[...]
        act_ref[1] = jnp.zeros(act_ref.shape[1:], act_ref.dtype)

    tile = jnp.minimum(j, last_tile)
    tt, d_model = y_ref.shape
    n_tg = tt // _LANES
    act_prev = act_ref[1 - cur]

    def v_side(c):
        dcols = slice(c * _MXU_COLS, (c + 1) * _MXU_COLS)
        y_ref[:, dcols] += jnp.dot(act_prev, v_ref[:, dcols], preferred_element_type=_F32)

    def u_side(half):
        u_rows = u_ref[half * _MXU_COLS:(half + 1) * _MXU_COLS, :]
        return lax.dot_general(xn_ref[...], u_rows, _NT, preferred_element_type=_F32)

    def gate(aa, tg):
        a = tile * a_per_tile + aa
        toks = slice(tg * _LANES, (tg + 1) * _LANES)
        gate_t = jnp.zeros((PEER_KEYS, _LANES), _F32)
        for h in range(PEER_HEADS):
            cand = p1_ref[h, pl.ds(a, 1), :][:, toks] + p2_ref[h, :, toks]
            pair = e1_ref[h, pl.ds(a, 1), :][:, toks] * e2_ref[h, :, toks]
            gate_t = gate_t + jnp.where(cand >= tau_ref[h, :, toks], pair, 0.0)
        gate_ref[toks, aa * PEER_KEYS:(aa + 1) * PEER_KEYS] = gate_t.T

    def activate(half, hidden):
        cols = slice(half * _MXU_COLS, (half + 1) * _MXU_COLS)
        act_ref[cur, :, cols] = (_gelu_tanh(hidden) * gate_ref[:, cols]).astype(act_ref.dtype)

    n_v = d_model // _MXU_COLS
    gates = [(aa, tg) for aa in range(a_per_tile) for tg in range(n_tg)]
    per_v = -(-len(gates) // (n_v - 2))
    hidden = [None, None]
    done = 0
    for c in range(n_v - 2):
        if c == 0:
            hidden[0] = u_side(0)
        if c == (n_v - 2) // 2:
            hidden[1] = u_side(1)
        v_side(c)
        for aa, tg in gates[done:done + per_v]:
            gate(aa, tg)
        done += per_v
    activate(0, hidden[0])
    v_side(n_v - 2)
    activate(1, hidden[1])
    v_side(n_v - 1)


def _experts(xn, u_bf16, v_bf16, p1, e1, p2, e2, tau, hp):
    T, D = xn.shape
    n_experts = u_bf16.shape[0]
    tt = _tile(T, 640, 128)
    a_per_tile = 4
    et = a_per_tile * PEER_KEYS
    assert et == 2 * _MXU_COLS and D % _MXU_COLS == 0 and D // _MXU_COLS > 2
    n_tiles = n_experts // et
    once = pl.Buffered(1)
    tok = pl.BlockSpec((PEER_HEADS, PEER_KEYS, tt), lambda i, j: (0, 0, i), pipeline_mode=once)
    return pl.pallas_call(
        functools.partial(_experts_body, a_per_tile=a_per_tile),
        out_shape=jax.ShapeDtypeStruct((T, D), _F32),
        grid=(T // tt, n_tiles + 1),
        in_specs=[
            pl.BlockSpec((tt, D), lambda i, j: (i, 0), pipeline_mode=once),
            pl.BlockSpec((et, D), lambda i, j: (jnp.minimum(j, n_tiles - 1), 0)),
            pl.BlockSpec((et, D), lambda i, j: (jnp.maximum(j - 1, 0), 0)),
            tok, tok, tok, tok,
            pl.BlockSpec((PEER_HEADS, 1, tt), lambda i, j: (0, 0, i)),
            pl.BlockSpec((tt, D), lambda i, j: (i, 0), pipeline_mode=once),
        ],
        out_specs=pl.BlockSpec((tt, D), lambda i, j: (i, 0)),
        scratch_shapes=[pltpu.VMEM((2, tt, et), _BF16), pltpu.VMEM((tt, et), _F32)],
        compiler_params=_params(("arbitrary", "arbitrary")),
        name="peer_experts",
    )(xn, u_bf16, v_bf16, p1, e1, p2, e2, tau, hp)


def _rope_tables(pos):
    half = HEAD_DIM // 2
    freqs = ROPE_BASE ** (-jnp.arange(half, dtype=_F32) / half)
    ang = pos.astype(_F32)[:, None] * freqs[None, :]
    cos, sin = jnp.cos(ang), jnp.sin(ang)
    return jnp.concatenate([cos, cos], axis=-1), jnp.concatenate([-sin, sin], axis=-1)


def kernel(x_prompt, x_sample, cache_k, cache_v, state_ret, page_table, norm1_gain, w_in, q_norm_gain,
           k_norm_gain, sb_bias, sb_out_gain, ret_out_gain, w_out, norm2_gain, peer_w_q, peer_subkeys,
           peer_u, peer_v):
    batch, seq, d_model = x_prompt.shape
    n_seq, n_new, _ = x_sample.shape
    depth, page, n_heads = cache_k.shape[0], cache_k.shape[2], cache_k.shape[3]
    width = n_heads * HEAD_DIM
    assert w_in.shape[2] == 7 * width and w_out.shape[1] == 2 * width
    assert n_heads * n_new % 8 == 0 and seq % (n_seq * n_new) == 0
    tp, ts = batch * seq, n_seq * n_new
    past = page_table.shape[1] * page

    pos = jnp.concatenate([jnp.tile(jnp.arange(seq), batch), jnp.tile(past + jnp.arange(n_new), n_seq)])
    cos_t, sin_t = _rope_tables(pos)
    log_g = jnp.log1p(-jnp.power(2.0, -5.0 - jnp.arange(n_heads, dtype=_F32)))

    h_all = jnp.concatenate([x_prompt.reshape(tp, d_model), x_sample.reshape(ts, d_model)], axis=0)
    outs = [[] for _ in range(6)]
    for l in range(depth):
        proj = _inproj(h_all, norm1_gain[l].reshape(1, d_model), w_in[l].astype(_BF16), cos_t, sin_t,
                       q_norm_gain[l], k_norm_gain[l], n_heads)
        proj_s = proj[tp:]
        oa_p = _sb_prompt(proj, sb_bias[l], sb_out_gain[l], batch, seq, n_heads)
        ob_p, ret_p = _ret_prompt(proj, log_g, ret_out_gain[l], batch, seq, n_heads)
        rows = lambda seg: proj_s[:, seg * width:(seg + 1) * width].reshape(n_seq, n_new * n_heads, HEAD_DIM)
        oa_s = _sb_decode(rows(0), rows(1), rows(2), sb_bias[l], sb_out_gain[l], cache_k, cache_v, l,
                          page_table, n_new)
        ob_s, ret_s = _ret_sample(proj_s, log_g, ret_out_gain[l], state_ret[l], n_seq, n_new, n_heads)
        o_a = jnp.concatenate([oa_p, oa_s.reshape(ts, width).astype(_BF16)], axis=0)
        o_b = jnp.concatenate([ob_p, ob_s], axis=0)
        hp, xn2 = _outproj(o_a, o_b, h_all, w_out[l].astype(_BF16), norm2_gain[l].reshape(1, d_model))
        p1, e1, p2, e2, tau = _router(xn2, peer_w_q[l].astype(_BF16), peer_subkeys[l])
        h_all = _experts(xn2, peer_u[l].astype(_BF16), peer_v[l].astype(_BF16), p1, e1, p2, e2, tau, hp)

        k_all, v_all = proj[:, width:2 * width], proj[:, 2 * width:3 * width]
        outs[0].append(k_all[:tp].reshape(batch, seq, n_heads, HEAD_DIM))
        outs[1].append(v_all[:tp].reshape(batch, seq, n_heads, HEAD_DIM))
        outs[2].append(ret_p)
        outs[3].append(k_all[tp:].reshape(n_seq, n_new, n_heads, HEAD_DIM))
        outs[4].append(v_all[tp:].reshape(n_seq, n_new, n_heads, HEAD_DIM))
        outs[5].append(ret_s)
    return (h_all[:tp].reshape(batch, seq, d_model), h_all[tp:].reshape(n_seq, n_new, d_model),
            *(jnp.stack(o) for o in outs))
```

```python
import functools

import jax
import jax.numpy as jnp
import numpy as np
from jax import lax
from jax.experimental import pallas as pl
from jax.experimental.pallas import tpu as pltpu

HEAD_DIM = 128
NORM_EPS = 1e-6
ROPE_BASE = 10000.0
PEER_HEADS = 8
PEER_KEYS = 128
PEER_TOPK = 16
PEER_HALF = 128
QK_SCALE = HEAD_DIM ** -0.5

_F32 = jnp.float32
_BF16 = jnp.bfloat16
_NEG_INF = float("-inf")
_V7X_VMEM_LIMIT_BYTES = 56 * 1024 * 1024

_NT = (((1,), (1,)), ((), ()))


def _params(sem):
    return pltpu.CompilerParams(dimension_semantics=sem, vmem_limit_bytes=_V7X_VMEM_LIMIT_BYTES)


def _tile(n, cap, mult):
    best = None
    for d in range(mult, min(n, cap) + 1, mult):
        if n % d == 0:
            best = d
    assert best is not None, (n, cap, mult)
    return best


LOG2_E = 1.4426950408889634


def _softplus2(z2):
    return jnp.maximum(z2, 0.0) + jnp.log2(1.0 + jnp.exp2(-jnp.abs(z2)))


def _strict_lower_ones(n):
    row = lax.broadcasted_iota(jnp.int32, (n, n), 0)
    col = lax.broadcasted_iota(jnp.int32, (n, n), 1)
    return jnp.where(row > col, 1.0, 0.0).astype(_BF16)


def _inproj_body(x_ref, g_ref, w_ref, cos_ref, sin_ref, qg_ref, kg_ref, o_ref, xn_ref, *, n_heads):
    j = pl.program_id(1)

    @pl.when(j == 0)
    def _():
        x = x_ref[...]
        ms = jnp.mean(x * x, axis=-1, keepdims=True)
        xn_ref[...] = (x * lax.rsqrt(ms + NORM_EPS) * g_ref[...]).astype(_BF16)

    acc = jnp.dot(xn_ref[...], w_ref[...], preferred_element_type=_F32)

    def per_head(fn):
        for h in range(n_heads):
            sl = slice(h * HEAD_DIM, (h + 1) * HEAD_DIM)
            o_ref[:, sl] = fn(acc[:, sl], h)

    def rms(a, gain):
        return a * lax.rsqrt(jnp.mean(a * a, axis=-1, keepdims=True) + NORM_EPS) * gain

    def rope(a):
        return a * cos_ref[...] + pltpu.roll(a, HEAD_DIM // 2, 1) * sin_ref[...]

    @pl.when(j == 0)
    def _():
        per_head(lambda a, h: rms(a, qg_ref[h:h + 1, :]) * (QK_SCALE * LOG2_E))

    @pl.when(j == 1)
    def _():
        per_head(lambda a, h: rms(a, kg_ref[h:h + 1, :]))

    @pl.when(j == 3)
    def _():
        per_head(lambda a, h: rope(a))

    @pl.when(j == 4)
    def _():
        per_head(lambda a, h: rope(a) * QK_SCALE)

    @pl.when((j == 2) | (j == 5) | (j == 6))
    def _():
        o_ref[...] = acc


def _inproj(x, gain, w_bf16, cos_t, sin_t, q_gain, k_gain, n_heads):
    T, D = x.shape
    width = n_heads * HEAD_DIM
    n_seg = w_bf16.shape[1] // width
    tm = _tile(T, 640, 8)
    return pl.pallas_call(
        functools.partial(_inproj_body, n_heads=n_heads),
        out_shape=jax.ShapeDtypeStruct((T, n_seg * width), _F32),
        grid=(T // tm, n_seg),
        in_specs=[
            pl.BlockSpec((tm, D), lambda i, j: (i, 0)),
            pl.BlockSpec((1, D), lambda i, j: (0, 0)),
            pl.BlockSpec((D, width), lambda i, j: (0, j)),
            pl.BlockSpec((tm, HEAD_DIM), lambda i, j: (i, 0)),
            pl.BlockSpec((tm, HEAD_DIM), lambda i, j: (i, 0)),
            pl.BlockSpec((n_heads, HEAD_DIM), lambda i, j: (0, 0)),
            pl.BlockSpec((n_heads, HEAD_DIM), lambda i, j: (0, 0)),
        ],
        out_specs=pl.BlockSpec((tm, width), lambda i, j: (i, j)),
        scratch_shapes=[pltpu.VMEM((tm, D), _BF16)],
        compiler_params=_params(("arbitrary", "arbitrary")),
        name="inproj",
    )(x, gain, w_bf16, cos_t, sin_t, q_gain, k_gain)


def _sb_blocks(chains, bias, tri):
    masked = lambda x, mask: x if mask is None else jnp.where(mask, x, 0.0)
    zs = [lax.dot_general(q, ks, _NT, preferred_element_type=_F32) + bias for q, ks, _, _, _, _ in chains]
    sps = [masked(_softplus2(z), c[5]) for z, c in zip(zs, chains)]
    laters = [jnp.dot(sp.astype(_BF16), tri, preferred_element_type=_F32) for sp in sps]
    out = []
    for z, sp, later, (_, _, vs, carry, acc, mask) in zip(zs, sps, laters, chains):
        w = masked(jnp.exp2(z - sp - (carry + later)), mask)
        acc = acc + jnp.dot(w.astype(_BF16), vs, preferred_element_type=_F32)
        out.append((carry + jnp.sum(sp, axis=1, keepdims=True), acc))
    return out


def _sb_prompt_body(bias_ref, q_ref, k_ref, v_ref, gain_ref, o_ref, *, blk):
    h = pl.program_id(1)
    i = pl.program_id(2)
    bias = bias_ref[h] * LOG2_E
    q_lo = q_ref[:blk, :].astype(_BF16)
    q_hi = q_ref[blk:, :].astype(_BF16)
    tri = _strict_lower_ones(blk)
    row = lax.broadcasted_iota(jnp.int32, (blk, blk), 0)
    col = lax.broadcasted_iota(jnp.int32, (blk, blk), 1)
    causal = col < row

    def kv(kb):
        start = pl.multiple_of(kb * blk, blk)
        return k_ref[pl.ds(start, blk), :].astype(_BF16), v_ref[pl.ds(start, blk), :].astype(_BF16)

    zero = (jnp.zeros((blk, 1), _F32), jnp.zeros((blk, HEAD_DIM), _F32))
    k1, v1 = kv(2 * i + 1)
    k0, v0 = kv(2 * i)
    lo, hi = _sb_blocks([(q_lo, k0, v0, *zero, causal), (q_hi, k1, v1, *zero, causal)], bias, tri)
    (hi,) = _sb_blocks([(q_hi, k0, v0, *hi, None)], bias, tri)

    def body(s, c):
        ks, vs = kv(2 * i - 1 - s)
        return tuple(_sb_blocks([(q_lo, ks, vs, *c[0], None), (q_hi, ks, vs, *c[1], None)], bias, tri))

    lo, hi = lax.fori_loop(0, 2 * i, body, (lo, hi))
    for part, (_, acc) in enumerate((lo, hi)):
        ms = jnp.mean(acc * acc, axis=-1, keepdims=True)
        o_ref[part * blk:(part + 1) * blk, :] = (acc * lax.rsqrt(ms + NORM_EPS) * gain_ref[...]).astype(o_ref.dtype)


def _sb_prompt(proj, sb_bias, sb_gain, batch, seq, n_heads):
    blk = _tile(seq // 2, 256, 128)
    nq = seq // (2 * blk)
    return pl.pallas_call(
        functools.partial(_sb_prompt_body, blk=blk),
        out_shape=jax.ShapeDtypeStruct((batch * seq, n_heads * HEAD_DIM), _BF16),
        grid=(batch, n_heads, nq),
        in_specs=[
            pl.BlockSpec(memory_space=pltpu.SMEM),
            pl.BlockSpec((2 * blk, HEAD_DIM), lambda b, h, i: (b * nq + i, h)),
            pl.BlockSpec((seq, HEAD_DIM), lambda b, h, i: (b, n_heads + h)),
            pl.BlockSpec((seq, HEAD_DIM), lambda b, h, i: (b, 2 * n_heads + h)),
            pl.BlockSpec((None, 1, HEAD_DIM), lambda b, h, i: (h, 0, 0)),
        ],
        out_specs=pl.BlockSpec((2 * blk, HEAD_DIM), lambda b, h, i: (b * nq + i, h)),
        compiler_params=_params(("arbitrary", "arbitrary", "arbitrary")),
        name="sb_prompt",
    )(sb_bias, proj, proj, proj, sb_gain.reshape(n_heads, 1, HEAD_DIM))


_LANES = 128
_MXU_COLS = 256


def _sb_flat_blocks(q, kfs, vfs, bias, tri_ones, carry, acc, valid):
    rows, groups = q.shape[0], kfs[0].shape[0] // _LANES
    zs = [lax.dot_general(q, kf, _NT, preferred_element_type=_F32) + bias for kf in kfs]
    sps = [jnp.where(valid, _softplus2(z), 0.0) for z in zs]
    boths = []
    for sp in sps:
        parts = [sp[:, g * _LANES:(g + 1) * _LANES] for g in range(groups)]
        stacked = parts[0] if groups == 1 else jnp.concatenate(parts, axis=0)
        boths.append(jnp.dot(stacked.astype(_BF16), tri_ones, preferred_element_type=_F32))
    ws = []
    for z, sp, both in zip(zs, sps, boths):
        later = [None] * groups
        for g in reversed(range(groups)):
            later[g] = both[g * rows:(g + 1) * rows, :_LANES] + carry
            carry = carry + both[g * rows:(g + 1) * rows, _LANES:]
        later = later[0] if groups == 1 else jnp.concatenate(later, axis=1)
        ws.append(jnp.where(valid, jnp.exp2(z - sp - later), 0.0).astype(_BF16))
    for w, vf in zip(ws, vfs):
        acc = acc + jnp.dot(w, vf, preferred_element_type=_F32)
    return carry, acc


def _sb_decode_body(pt_ref, q_ref, bias_ref, tq_ref, hq_ref, hk_ref, hkn_ref, keyn_ref, knew_ref, vnew_ref,
                    gain_ref, *rest, pages_per_step):
    del pt_ref
    kv_refs = rest[:2 * pages_per_step]
    o_ref, carry_ref, acc_ref = rest[2 * pages_per_step:]
    s = pl.program_id(1)
    q = q_ref[...].astype(_BF16)
    bias = bias_ref[...]
    row = lax.broadcasted_iota(jnp.int32, (_LANES, 2 * _LANES), 0)
    col = lax.broadcasted_iota(jnp.int32, (_LANES, 2 * _LANES), 1)
    tri_ones = jnp.where(col >= _LANES, 1.0, jnp.where(row > col, 1.0, 0.0)).astype(_BF16)
    own = hq_ref[...] == hk_ref[...]

    @pl.when(s == 0)
    def _():
        never = jnp.iinfo(jnp.int32).max
        valid = jnp.where(hq_ref[...] == hkn_ref[...], keyn_ref[...], never) < tq_ref[...]
        carry, acc = _sb_flat_blocks(q, [knew_ref[...].astype(_BF16)], [vnew_ref[...].astype(_BF16)], bias,
                                     tri_ones, jnp.zeros(carry_ref.shape, _F32), jnp.zeros(acc_ref.shape, _F32),
                                     valid)
        carry_ref[...] = carry
        acc_ref[...] = acc

    def flat(ref):
        page = ref[...]
        return page.reshape(page.shape[0] * page.shape[1], page.shape[2]).astype(_BF16)

    carry, acc = _sb_flat_blocks(q, [flat(r) for r in kv_refs[0::2]], [flat(r) for r in kv_refs[1::2]], bias,
                                 tri_ones, carry_ref[...], acc_ref[...], own)
    carry_ref[...] = carry
    acc_ref[...] = acc

    @pl.when(s == pl.num_programs(1) - 1)
    def _():
        ms = jnp.mean(acc * acc, axis=-1, keepdims=True)
        o_ref[...] = acc * lax.rsqrt(ms + NORM_EPS) * gain_ref[...]


def _sb_decode(q_rows, k_new, v_new, sb_bias, sb_gain, cache_k, cache_v, layer, page_table, n_new):
    n_seq, n_rows, hd = q_rows.shape
    page, n_heads = cache_k.shape[2], cache_k.shape[3]
    n_pages = page_table.shape[1]
    pps = _tile(n_pages, 16, 1)
    n_steps = n_pages // pps
    assert n_rows <= _LANES and _LANES % n_heads == 0

    i32 = jnp.int32
    col = lambda v: v.reshape(n_rows, 1)
    bias_col = col(jnp.tile(sb_bias, n_new)) * LOG2_E
    tq_col = col(jnp.repeat(jnp.arange(n_new, dtype=i32), n_heads))
    hq_col = col(jnp.tile(jnp.arange(n_heads, dtype=i32), n_new))
    hk_row = jnp.tile(jnp.arange(n_heads, dtype=i32), page).reshape(1, page * n_heads)
    hkn_row = jnp.tile(jnp.arange(n_heads, dtype=i32), _LANES // n_heads).reshape(1, _LANES)
    keyn_row = jnp.repeat(jnp.arange(_LANES // n_heads, dtype=i32), n_heads).reshape(1, _LANES)
    gain_rows = jnp.tile(sb_gain, (n_new, 1))
    pad = lambda t: jnp.pad(t, ((0, 0), (0, _LANES - n_rows), (0, 0)))

    def page_map(r):
        return lambda b, s, pt: (layer, pt[b, n_pages - 1 - (s * pps + r)], 0, 0, 0)

    kv_specs, kv_args = [], []
    for r in range(pps):
        kv_specs += [pl.BlockSpec((None, None, page, n_heads, hd), page_map(r))] * 2
        kv_args += [cache_k, cache_v]
    seq_map = lambda b, s, pt: (b, 0, 0)
    fix_map = lambda b, s, pt: (0, 0)
    fixed = lambda a: pl.BlockSpec(a.shape, fix_map)
    small = [bias_col, tq_col, hq_col, hk_row, hkn_row, keyn_row]
    grid_spec = pltpu.PrefetchScalarGridSpec(
        num_scalar_prefetch=1,
        grid=(n_seq, n_steps),
        in_specs=[pl.BlockSpec((None, n_rows, hd), seq_map)] + [fixed(a) for a in small] + [
            pl.BlockSpec((None, _LANES, hd), seq_map),
            pl.BlockSpec((None, _LANES, hd), seq_map),
            fixed(gain_rows),
        ] + kv_specs,
        out_specs=pl.BlockSpec((None, n_rows, hd), seq_map),
        scratch_shapes=[pltpu.VMEM((n_rows, _LANES), _F32), pltpu.VMEM((n_rows, hd), _F32)],
    )
    return pl.pallas_call(
        functools.partial(_sb_decode_body, pages_per_step=pps),
        out_shape=jax.ShapeDtypeStruct((n_seq, n_rows, hd), _F32),
        grid_spec=grid_spec,
        compiler_params=_params(("arbitrary", "arbitrary")),
        name="sb_decode",
    )(page_table, q_rows, *small, pad(k_new), pad(v_new), gain_rows, *kv_args)


def _gated_group_norm(o, gain, gate):
    mu = jnp.mean(o, axis=-1, keepdims=True)
    d = o - mu
    var = jnp.mean(d * d, axis=-1, keepdims=True)
    return d * lax.rsqrt(var + NORM_EPS) * gain * (gate * jax.nn.sigmoid(gate))


def _ret_prompt_body(logg_ref, q_ref, k_ref, v_ref, g_ref, gain_ref, o_ref, s_ref, state_ref, decay_ref, *, chunk,
                     heads):
    hg = pl.program_id(1)
    c = pl.program_id(2)

    lgs = [logg_ref[hg * heads + i] for i in range(heads)]

    @pl.when(c == 0)
    def _():
        state_ref[...] = jnp.zeros_like(state_ref)
        li = lax.broadcasted_iota(jnp.int32, (chunk, chunk), 0)
        mi = lax.broadcasted_iota(jnp.int32, (chunk, chunk), 1)
        dist = (li - mi).astype(_F32)
        for i in range(heads):
            decay_ref[i] = jnp.where(dist >= 0, jnp.exp(jnp.maximum(dist, 0.0) * lgs[i]), 0.0)

    pos = lax.broadcasted_iota(jnp.int32, (chunk, 1), 0).astype(_F32)
    full = jnp.full((1, HEAD_DIM), float(chunk), _F32)
    cols = [slice(i * HEAD_DIM, (i + 1) * HEAD_DIM) for i in range(heads)]
    qs = [q_ref[:, cl].astype(_BF16) for cl in cols]
    ks = [k_ref[:, cl] for cl in cols]
    vs = [v_ref[:, cl].astype(_BF16) for cl in cols]
    states = [state_ref[i] for i in range(heads)]
    scores = [lax.dot_general(q, k.astype(_BF16), _NT, preferred_element_type=_F32) for q, k in zip(qs, ks)]
    from_state = [jnp.dot(q, s.astype(_BF16), preferred_element_type=_F32) for q, s in zip(qs, states)]
    updates = [jnp.dot((k * jnp.exp((chunk - 1.0 - pos) * lg)).T.astype(_BF16), v, preferred_element_type=_F32)
               for k, v, lg in zip(ks, vs, lgs)]
    decayed = [(sc * decay_ref[i]).astype(_BF16) for i, sc in enumerate(scores)]
    outs = [jnp.dot(d, v, preferred_element_type=_F32) for d, v in zip(decayed, vs)]
    last = c == pl.num_programs(2) - 1
    for i in range(heads):
        new_state = states[i] * jnp.exp(full * lgs[i]) + updates[i]
        state_ref[i] = new_state

        @pl.when(last)
        def _():
            s_ref[i] = new_state

        o = outs[i] + from_state[i] * jnp.exp((pos + 1.0) * lgs[i])
        o_ref[:, cols[i]] = _gated_group_norm(o, gain_ref[i], g_ref[:, cols[i]]).astype(o_ref.dtype)


def _ret_prompt(proj, log_g, ret_gain, batch, seq, n_heads):
    chunk = _tile(seq, 256, 128)
    nc = seq // chunk
    heads = _tile(n_heads, 4, 1)
    groups = n_heads // heads
    width = heads * HEAD_DIM
    col = lambda seg: (lambda b, hg, c: (b * nc + c, seg * groups + hg))
    return pl.pallas_call(
        functools.partial(_ret_prompt_body, chunk=chunk, heads=heads),
        out_shape=(jax.ShapeDtypeStruct((batch * seq, n_heads * HEAD_DIM), _BF16),
                   jax.ShapeDtypeStruct((batch, n_heads, HEAD_DIM, HEAD_DIM), _F32)),
        grid=(batch, groups, nc),
        in_specs=[
            pl.BlockSpec(memory_space=pltpu.SMEM),
            pl.BlockSpec((chunk, width), col(3)),
            pl.BlockSpec((chunk, width), col(4)),
            pl.BlockSpec((chunk, width), col(5)),
            pl.BlockSpec((chunk, width), col(6)),
            pl.BlockSpec((heads, 1, HEAD_DIM), lambda b, hg, c: (hg, 0, 0)),
        ],
        out_specs=(pl.BlockSpec((chunk, width), lambda b, hg, c: (b * nc + c, hg)),
                   pl.BlockSpec((None, heads, HEAD_DIM, HEAD_DIM), lambda b, hg, c: (b, hg, 0, 0))),
        scratch_shapes=[pltpu.VMEM((heads, HEAD_DIM, HEAD_DIM), _F32), pltpu.VMEM((heads, chunk, chunk), _F32)],
        compiler_params=_params(("arbitrary", "arbitrary", "arbitrary")),
        name="ret_prompt",
    )(log_g, proj, proj, proj, proj, ret_gain.reshape(n_heads, 1, HEAD_DIM))


def _ret_sample_body(logg_ref, q_ref, k_ref, v_ref, g_ref, gain_ref, seqc_ref, seqr_ref, posc_ref, posr_ref,
                     s0_ref, o_ref, s_ref, *, n_new):
    h = pl.program_id(0)
    lg = logg_ref[h]
    n_seq = s0_ref.shape[0]
    q = q_ref[...].astype(_BF16)
    k = k_ref[...]
    v = v_ref[...].astype(_BF16)
    seq_c, seq_r = seqc_ref[...], seqr_ref[...]
    pos_c, pos_r = posc_ref[...], posr_ref[...]
    dist = pos_c - pos_r
    decay = jnp.where(seq_c == seq_r, jnp.where(dist >= 0, jnp.exp(jnp.maximum(dist, 0.0) * lg), 0.0), 0.0)
    scores = lax.dot_general(q, k.astype(_BF16), _NT, preferred_element_type=_F32) * decay
    o = jnp.dot(scores.astype(_BF16), v, preferred_element_type=_F32)
    q_decay = jnp.exp((pos_c + 1.0) * lg)
    k_tail_t = k.T * jnp.exp((n_new - 1.0 - pos_r) * lg)
    full = jnp.full((1, HEAD_DIM), float(n_new), _F32)
    state_decay = jnp.exp(full * lg)

    def body(b, o):
        state = s0_ref[b]
        from_state = jnp.dot(q, state.astype(_BF16), preferred_element_type=_F32) * q_decay
        o = o + jnp.where(seq_c == b, from_state, 0.0)
        kt = jnp.where(seq_r == b, k_tail_t, 0.0).astype(_BF16)
        s_ref[b] = state * state_decay + jnp.dot(kt, v, preferred_element_type=_F32)
        return o

    o = lax.fori_loop(0, n_seq, body, o)
    o_ref[...] = _gated_group_norm(o, gain_ref[...], g_ref[...]).astype(o_ref.dtype)


def _ret_sample(proj_s, log_g, ret_gain, state, n_seq, n_new, n_heads):
    ts = n_seq * n_new
    seq_id = jnp.repeat(jnp.arange(n_seq, dtype=jnp.int32), n_new)
    pos = jnp.tile(jnp.arange(n_new, dtype=_F32), n_seq)
    col = lambda seg: (lambda h: (0, seg * n_heads + h))
    fix = lambda h: (0, 0)
    return pl.pallas_call(
        functools.partial(_ret_sample_body, n_new=n_new),
        out_shape=(jax.ShapeDtypeStruct((ts, n_heads * HEAD_DIM), _BF16),
                   jax.ShapeDtypeStruct((n_seq, n_heads, HEAD_DIM, HEAD_DIM), _F32)),
        grid=(n_heads,),
        in_specs=[
            pl.BlockSpec(memory_space=pltpu.SMEM),
            pl.BlockSpec((ts, HEAD_DIM), col(3)),
            pl.BlockSpec((ts, HEAD_DIM), col(4)),
            pl.BlockSpec((ts, HEAD_DIM), col(5)),
            pl.BlockSpec((ts, HEAD_DIM), col(6)),
            pl.BlockSpec((None, 1, HEAD_DIM), lambda h: (h, 0, 0)),
            pl.BlockSpec((ts, 1), fix),
            pl.BlockSpec((1, ts), fix),
            pl.BlockSpec((ts, 1), fix),
            pl.BlockSpec((1, ts), fix),
            pl.BlockSpec((n_seq, None, HEAD_DIM, HEAD_DIM), lambda h: (0, h, 0, 0)),
        ],
        out_specs=(pl.BlockSpec((ts, HEAD_DIM), lambda h: (0, h)),
                   pl.BlockSpec((n_seq, None, HEAD_DIM, HEAD_DIM), lambda h: (0, h, 0, 0))),
        compiler_params=_params(("arbitrary",)),
        name="ret_sample",
    )(log_g, proj_s, proj_s, proj_s, proj_s, ret_gain.reshape(n_heads, 1, HEAD_DIM),
      seq_id.reshape(ts, 1), seq_id.reshape(1, ts), pos.reshape(ts, 1), pos.reshape(1, ts), state)


def _outproj_body(oa_ref, ob_ref, x_ref, w_ref, g_ref, hp_ref, xn_ref):
    half = oa_ref.shape[1]
    hp = (x_ref[...]
          + jnp.dot(oa_ref[...], w_ref[:half, :], preferred_element_type=_F32)
          + jnp.dot(ob_ref[...], w_ref[half:, :], preferred_element_type=_F32))
    hp_ref[...] = hp
    ms = jnp.mean(hp * hp, axis=-1, keepdims=True)
    xn_ref[...] = (hp * lax.rsqrt(ms + NORM_EPS) * g_ref[...]).astype(xn_ref.dtype)


def _outproj(o_a, o_b, x, w_bf16, gain2):
    T, D = x.shape
    half = o_a.shape[1]
    tm = _tile(T, 320, 16)
    return pl.pallas_call(
        _outproj_body,
        out_shape=(jax.ShapeDtypeStruct((T, D), _F32), jax.ShapeDtypeStruct((T, D), _BF16)),
        grid=(T // tm,),
        in_specs=[
            pl.BlockSpec((tm, half), lambda i: (i, 0)),
            pl.BlockSpec((tm, half), lambda i: (i, 0)),
            pl.BlockSpec((tm, D), lambda i: (i, 0)),
            pl.BlockSpec((2 * half, D), lambda i: (0, 0)),
            pl.BlockSpec((1, D), lambda i: (0, 0)),
        ],
        out_specs=(pl.BlockSpec((tm, D), lambda i: (i, 0)), pl.BlockSpec((tm, D), lambda i: (i, 0))),
        compiler_params=_params(("arbitrary",)),
        name="outproj",
    )(o_a, o_b, x, w_bf16, gain2)


def _top_values(s, n):
    vals = []
    cur = s
    for _ in range(n):
        m = jnp.max(cur, axis=0, keepdims=True)
        vals.append(m)
        cur = jnp.where(cur == m, _NEG_INF, cur)
    return vals


def _stack_rows(rows):
    n = len(rows)
    ridx = lax.broadcasted_iota(jnp.int32, (n, rows[0].shape[1]), 0)
    out = jnp.broadcast_to(rows[0], ridx.shape)
    for i in range(1, n):
        out = jnp.where(ridx == i, rows[i], out)
    return out


def _router_body(xn_ref, wq_ref, sk_ref, p1_ref, e1_ref, p2_ref, e2_ref, tau_ref, q_ref):
    k = PEER_TOPK
    q_ref[...] = jnp.dot(xn_ref[...], wq_ref[...], preferred_element_type=_F32)
    tt = xn_ref.shape[0]
    ridx = lax.broadcasted_iota(jnp.int32, (k, tt), 0)

    def head(h, _):
        def scores(c):
            start = pl.multiple_of((2 * h + c) * PEER_HALF, PEER_HALF)
            qc = q_ref[:, pl.ds(start, PEER_HALF)]
            sk = sk_ref[h, c]
            q_hi = qc.astype(_BF16)
            q_lo = (qc - q_hi.astype(_F32)).astype(_BF16)
            sk_hi = sk.astype(_BF16)
            sk_lo = (sk - sk_hi.astype(_F32)).astype(_BF16)
            dot = lambda a, b: lax.dot_general(a, b, _NT, preferred_element_type=_F32)
            return dot(sk_hi, q_hi) + (dot(sk_hi, q_lo) + dot(sk_lo, q_hi))

        s1, s2 = scores(0), scores(1)
        t1, t2 = _top_values(s1, k), _top_values(s2, k)
        t1s, t2s = _stack_rows(t1), _stack_rows(t2)
        cands = [jnp.where(ridx < k // (i + 1), t1[i] + t2s, _NEG_INF) for i in range(k // 2)]
        cands.append(jnp.where(ridx >= k // 2, t1s + t2[0], _NEG_INF))
        top = t1[0] + t2[0]
        z = jnp.zeros_like(top)
        tau = top
        for _ in range(k):
            m = cands[0]
            for cnd in cands[1:]:
                m = jnp.maximum(m, cnd)
            m = jnp.max(m, axis=0, keepdims=True)
            z = z + jnp.exp(m - top)
            tau = m
            cands = [jnp.where(cnd == m, _NEG_INF, cnd) for cnd in cands]
        p1_ref[h] = jnp.where(s1 >= t1[k - 1], s1, _NEG_INF)
        p2_ref[h] = jnp.where(s2 >= t2[k - 1], s2, _NEG_INF)
        e1_ref[h] = jnp.exp(s1 - t1[0])
        e2_ref[h] = jnp.exp(s2 - t2[0]) / z
        tau_ref[h] = tau
        return 0

    lax.fori_loop(0, PEER_HEADS, head, 0)


def _router(xn, wq_bf16, subkeys):
    T, D = xn.shape
    tt = _tile(T, 640, 128)
    big = jax.ShapeDtypeStruct((PEER_HEADS, PEER_KEYS, T), _F32)
    big_spec = pl.BlockSpec((PEER_HEADS, PEER_KEYS, tt), lambda i: (0, 0, i))
    return pl.pallas_call(
        _router_body,
        out_shape=(big, big, big, big, jax.ShapeDtypeStruct((PEER_HEADS, 1, T), _F32)),
        grid=(T // tt,),
        in_specs=[
            pl.BlockSpec((tt, D), lambda i: (i, 0)),
            pl.BlockSpec(wq_bf16.shape, lambda i: (0, 0)),
            pl.BlockSpec(subkeys.shape, lambda i: (0, 0, 0, 0)),
        ],
        out_specs=(big_spec, big_spec, big_spec, big_spec,
                   pl.BlockSpec((PEER_HEADS, 1, tt), lambda i: (0, 0, i))),
        scratch_shapes=[pltpu.VMEM((tt, wq_bf16.shape[1]), _F32)],
        compiler_params=_params(("arbitrary",)),
        name="peer_router",
    )(xn, wq_bf16, subkeys)


def _gelu_tanh(x):
    c = 0.7978845608028654
    return x * (0.5 * (1.0 + jnp.tanh(c * (x + 0.044715 * (x * x * x)))))


def _experts_step(xn_ref, u_ref, v_ref, p1_ref, e1_ref, p2_ref, e2_ref, tau_ref, y_ref, act_ref, hid_in, hid_out,
                  first_a, a_per_tile):
    tt = y_ref.shape[0]
    for half in range(a_per_tile // 2):
        cols = slice(half * _MXU_COLS, (half + 1) * _MXU_COLS)
        for aa in range(2 * half, 2 * half + 2):
            a = first_a + aa
            acols = slice(aa * PEER_KEYS, (aa + 1) * PEER_KEYS)
            p1_rows = [p1_ref[h, pl.ds(a, 1), :] for h in range(PEER_HEADS)]
            e1_rows = [e1_ref[h, pl.ds(a, 1), :] for h in range(PEER_HEADS)]
            for tg in range(tt // _LANES):
                toks = slice(tg * _LANES, (tg + 1) * _LANES)
                gate_t = jnp.zeros((PEER_KEYS, _LANES), _F32)
                for h in range(PEER_HEADS):
                    cand = p1_rows[h][:, toks] + p2_ref[h, :, toks]
                    pair = e1_rows[h][:, toks] * e2_ref[h, :, toks]
                    gate_t = gate_t + jnp.where(cand >= tau_ref[h, :, toks], pair, 0.0)
                act_ref[toks, acols] = (_gelu_tanh(hid_in[toks, acols]) * gate_t.T).astype(act_ref.dtype)
        y_ref[...] += jnp.dot(act_ref[:, cols], v_ref[cols, :], preferred_element_type=_F32)
    hid_out[...] = lax.dot_general(xn_ref[...], u_ref[...], _NT, preferred_element_type=_F32)


def _experts_body(xn_ref, u_ref, v_ref, p1_ref, e1_ref, p2_ref, e2_ref, tau_ref, hp_ref, y_ref, act_ref, hid_a,
                  hid_b, *, a_per_tile):
    j = pl.program_id(1)
    first_a = (j - 1) * a_per_tile
    step = functools.partial(_experts_step, xn_ref, u_ref, v_ref, p1_ref, e1_ref, p2_ref, e2_ref, tau_ref, y_ref,
                             act_ref)

    @pl.when(j == 0)
    def _():
        y_ref[...] = hp_ref[...]
        hid_a[...] = lax.dot_general(xn_ref[...], u_ref[...], _NT, preferred_element_type=_F32)

    @pl.when(j % 2 == 1)
    def _():
        step(hid_a, hid_b, first_a, a_per_tile)

    @pl.when((j % 2 == 0) & (j > 0))
    def _():
        step(hid_b, hid_a, first_a, a_per_tile)


def _experts(xn, u_bf16, v_bf16, p1, e1, p2, e2, tau, hp):
    T, D = xn.shape
    n_experts = u_bf16.shape[0]
    tt = _tile(T, 640, 128)
    a_per_tile = 4
    et = a_per_tile * PEER_KEYS
    assert et == 2 * _MXU_COLS
    n_tiles = n_experts // et
    once = pl.Buffered(1)
    tok = pl.BlockSpec((PEER_HEADS, PEER_KEYS, tt), lambda i, j: (0, 0, i), pipeline_mode=once)
    return pl.pallas_call(
        functools.partial(_experts_body, a_per_tile=a_per_tile),
        out_shape=jax.ShapeDtypeStruct((T, D), _F32),
        grid=(T // tt, n_tiles + 1),
        in_specs=[
            pl.BlockSpec((tt, D), lambda i, j: (i, 0), pipeline_mode=once),
            pl.BlockSpec((et, D), lambda i, j: (jnp.minimum(j, n_tiles - 1), 0)),
            pl.BlockSpec((et, D), lambda i, j: (jnp.maximum(j - 1, 0), 0)),
            tok, tok, tok, tok,
            pl.BlockSpec((PEER_HEADS, 1, tt), lambda i, j: (0, 0, i)),
            pl.BlockSpec((tt, D), lambda i, j: (i, 0), pipeline_mode=once),
        ],
        out_specs=pl.BlockSpec((tt, D), lambda i, j: (i, 0)),
        scratch_shapes=[pltpu.VMEM((tt, et), _BF16), pltpu.VMEM((tt, et), _F32), pltpu.VMEM((tt, et), _F32)],
        compiler_params=_params(("arbitrary", "arbitrary")),
        name="peer_experts",
    )(xn, u_bf16, v_bf16, p1, e1, p2, e2, tau, hp)


def _rope_tables(pos):
    half = HEAD_DIM // 2
    freqs = ROPE_BASE ** (-jnp.arange(half, dtype=_F32) / half)
    ang = jnp.asarray(pos, _F32)[:, None] * freqs[None, :]
    cos, sin = jnp.cos(ang), jnp.sin(ang)
    return jnp.concatenate([cos, cos], axis=-1), jnp.concatenate([-sin, sin], axis=-1)


def kernel(x_prompt, x_sample, cache_k, cache_v, state_ret, page_table, norm1_gain, w_in, q_norm_gain,
           k_norm_gain, sb_bias, sb_out_gain, ret_out_gain, w_out, norm2_gain, peer_w_q, peer_subkeys,
           peer_u, peer_v):
    batch, seq, d_model = x_prompt.shape
    n_seq, n_new, _ = x_sample.shape
    depth, page, n_heads = cache_k.shape[0], cache_k.shape[2], cache_k.shape[3]
    width = n_heads * HEAD_DIM
    assert w_in.shape[2] == 7 * width and w_out.shape[1] == 2 * width
    assert n_heads * n_new % 8 == 0 and seq % (n_seq * n_new) == 0
    tp, ts = batch * seq, n_seq * n_new
    past = page_table.shape[1] * page

    pos = np.concatenate([np.tile(np.arange(seq), batch), np.tile(past + np.arange(n_new), n_seq)])
    cos_t, sin_t = _rope_tables(pos)
    log_g = jnp.log1p(-jnp.power(2.0, -5.0 - jnp.arange(n_heads, dtype=_F32)))

    h_all = jnp.concatenate([x_prompt.reshape(tp, d_model), x_sample.reshape(ts, d_model)], axis=0)
    outs = [[] for _ in range(6)]
    for l in range(depth):
        proj = _inproj(h_all, norm1_gain[l].reshape(1, d_model), w_in[l].astype(_BF16), cos_t, sin_t,
                       q_norm_gain[l], k_norm_gain[l], n_heads)
        proj_s = proj[tp:]
        oa_p = _sb_prompt(proj, sb_bias[l], sb_out_gain[l], batch, seq, n_heads)
        ob_p, ret_p = _ret_prompt(proj, log_g, ret_out_gain[l], batch, seq, n_heads)
        rows = lambda seg: proj_s[:, seg * width:(seg + 1) * width].reshape(n_seq, n_new * n_heads, HEAD_DIM)
        oa_s = _sb_decode(rows(0), rows(1), rows(2), sb_bias[l], sb_out_gain[l], cache_k, cache_v, l,
                          page_table, n_new)
        ob_s, ret_s = _ret_sample(proj_s, log_g, ret_out_gain[l], state_ret[l], n_seq, n_new, n_heads)
        o_a = jnp.concatenate([oa_p, oa_s.reshape(ts, width).astype(_BF16)], axis=0)
        o_b = jnp.concatenate([ob_p, ob_s], axis=0)
        hp, xn2 = _outproj(o_a, o_b, h_all, w_out[l].astype(_BF16), norm2_gain[l].reshape(1, d_model))
        p1, e1, p2, e2, tau = _router(xn2, peer_w_q[l].astype(_BF16), peer_subkeys[l])
        h_all = _experts(xn2, peer_u[l].astype(_BF16), peer_v[l].astype(_BF16), p1, e1, p2, e2, tau, hp)

        k_all, v_all = proj[:, width:2 * width], proj[:, 2 * width:3 * width]
        outs[0].append(k_all[:tp].reshape(batch, seq, n_heads, HEAD_DIM))
        outs[1].append(v_all[:tp].reshape(batch, seq, n_heads, HEAD_DIM))
        outs[2].append(ret_p)
        outs[3].append(k_all[tp:].reshape(n_seq, n_new, n_heads, HEAD_DIM))
        outs[4].append(v_all[tp:].reshape(n_seq, n_new, n_heads, HEAD_DIM))
        outs[5].append(ret_s)
    return (h_all[:tp].reshape(batch, seq, d_model), h_all[tp:].reshape(n_seq, n_new, d_model),
            *(jnp.stack(o) for o in outs))
```

```python
import functools

import jax
import jax.numpy as jnp
import numpy as np
from jax import lax
from jax.experimental import pallas as pl
from jax.experimental.pallas import tpu as pltpu

HEAD_DIM = 128
NORM_EPS = 1e-6
ROPE_BASE = 10000.0
PEER_HEADS = 8
PEER_KEYS = 128
PEER_TOPK = 16
PEER_HALF = 128
QK_SCALE = HEAD_DIM ** -0.5

_F32 = jnp.float32
_BF16 = jnp.bfloat16
_NEG_INF = float("-inf")
_V7X_VMEM_LIMIT_BYTES = 56 * 1024 * 1024

_NT = (((1,), (1,)), ((), ()))


def _params(sem):
    return pltpu.CompilerParams(dimension_semantics=sem, vmem_limit_bytes=_V7X_VMEM_LIMIT_BYTES)


def _tile(n, cap, mult):
    best = None
    for d in range(mult, min(n, cap) + 1, mult):
        if n % d == 0:
            best = d
    assert best is not None, (n, cap, mult)
    return best


LOG2_E = 1.4426950408889634


def _softplus2(z2):
    return jnp.maximum(z2, 0.0) + jnp.log2(1.0 + jnp.exp2(-jnp.abs(z2)))


def _strict_lower_ones(n):
    row = lax.broadcasted_iota(jnp.int32, (n, n), 0)
    col = lax.broadcasted_iota(jnp.int32, (n, n), 1)
    return jnp.where(row > col, 1.0, 0.0).astype(_BF16)


def _inproj_body(x_ref, g_ref, w_ref, cos_ref, sin_ref, qg_ref, kg_ref, o_ref, xn_ref, *, n_heads):
    j = pl.program_id(1)

    @pl.when(j == 0)
    def _():
        x = x_ref[...]
        ms = jnp.mean(x * x, axis=-1, keepdims=True)
        xn_ref[...] = (x * lax.rsqrt(ms + NORM_EPS) * g_ref[...]).astype(_BF16)

    acc = jnp.dot(xn_ref[...], w_ref[...], preferred_element_type=_F32)

    def per_head(fn):
        for h in range(n_heads):
            sl = slice(h * HEAD_DIM, (h + 1) * HEAD_DIM)
            o_ref[:, sl] = fn(acc[:, sl], h)

    def rms(a, gain):
        return a * lax.rsqrt(jnp.mean(a * a, axis=-1, keepdims=True) + NORM_EPS) * gain

    def rope(a):
        return a * cos_ref[...] + pltpu.roll(a, HEAD_DIM // 2, 1) * sin_ref[...]

    @pl.when(j == 0)
    def _():
        per_head(lambda a, h: rms(a, qg_ref[h:h + 1, :]) * (QK_SCALE * LOG2_E))

    @pl.when(j == 1)
    def _():
        per_head(lambda a, h: rms(a, kg_ref[h:h + 1, :]))

    @pl.when(j == 3)
    def _():
        per_head(lambda a, h: rope(a))

    @pl.when(j == 4)
    def _():
        per_head(lambda a, h: rope(a) * QK_SCALE)

    @pl.when((j == 2) | (j == 5) | (j == 6))
    def _():
        o_ref[...] = acc


def _inproj(x, gain, w_bf16, cos_t, sin_t, q_gain, k_gain, n_heads):
    T, D = x.shape
    width = n_heads * HEAD_DIM
    n_seg = w_bf16.shape[1] // width
    tm = _tile(T, 640, 8)
    return pl.pallas_call(
        functools.partial(_inproj_body, n_heads=n_heads),
        out_shape=jax.ShapeDtypeStruct((T, n_seg * width), _F32),
        grid=(T // tm, n_seg),
        in_specs=[
            pl.BlockSpec((tm, D), lambda i, j: (i, 0)),
            pl.BlockSpec((1, D), lambda i, j: (0, 0)),
            pl.BlockSpec((D, width), lambda i, j: (0, j)),
            pl.BlockSpec((tm, HEAD_DIM), lambda i, j: (i, 0)),
            pl.BlockSpec((tm, HEAD_DIM), lambda i, j: (i, 0)),
            pl.BlockSpec((n_heads, HEAD_DIM), lambda i, j: (0, 0)),
            pl.BlockSpec((n_heads, HEAD_DIM), lambda i, j: (0, 0)),
        ],
        out_specs=pl.BlockSpec((tm, width), lambda i, j: (i, j)),
        scratch_shapes=[pltpu.VMEM((tm, D), _BF16)],
        compiler_params=_params(("arbitrary", "arbitrary")),
        name="inproj",
    )(x, gain, w_bf16, cos_t, sin_t, q_gain, k_gain)


def _sb_blocks(chains, tri):
    masked = lambda x, mask: x if mask is None else jnp.where(mask, x, 0.0)
    zs = [lax.dot_general(c[0], c[1], _NT, preferred_element_type=_F32) + c[3] for c in chains]
    sps = [masked(_softplus2(z), c[6]) for z, c in zip(zs, chains)]
    laters = [jnp.dot(sp.astype(_BF16), tri, preferred_element_type=_F32) for sp in sps]
    out = []
    for z, sp, later, (_, _, vs, _, carry, acc, mask) in zip(zs, sps, laters, chains):
        w = masked(jnp.exp2(z - sp - (carry + later)), mask)
        acc = acc + jnp.dot(w.astype(_BF16), vs, preferred_element_type=_F32)
        out.append((carry + jnp.sum(sp, axis=1, keepdims=True), acc))
    return out


def _sb_prompt_body(bias_ref, q_ref, k_ref, v_ref, gain_ref, o_ref, *, blk, heads):
    hg = pl.program_id(1)
    i = pl.program_id(2)
    tri = _strict_lower_ones(blk)
    row = lax.broadcasted_iota(jnp.int32, (blk, blk), 0)
    col = lax.broadcasted_iota(jnp.int32, (blk, blk), 1)
    causal = col < row
    cols = [slice(hh * HEAD_DIM, (hh + 1) * HEAD_DIM) for hh in range(heads)]
    biases = [bias_ref[hg * heads + hh] * LOG2_E for hh in range(heads)]
    q_lo = [q_ref[:blk, cl].astype(_BF16) for cl in cols]
    q_hi = [q_ref[blk:, cl].astype(_BF16) for cl in cols]

    def kv(kb):
        start = pl.multiple_of(kb * blk, blk)
        return ([k_ref[pl.ds(start, blk), cl].astype(_BF16) for cl in cols],
                [v_ref[pl.ds(start, blk), cl].astype(_BF16) for cl in cols])

    zero = (jnp.zeros((blk, 1), _F32), jnp.zeros((blk, HEAD_DIM), _F32))
    k1, v1 = kv(2 * i + 1)
    k0, v0 = kv(2 * i)
    state = _sb_blocks([(q_lo[hh], k0[hh], v0[hh], biases[hh], *zero, causal) for hh in range(heads)]
                       + [(q_hi[hh], k1[hh], v1[hh], biases[hh], *zero, causal) for hh in range(heads)], tri)
    lo, hi = state[:heads], state[heads:]
    hi = _sb_blocks([(q_hi[hh], k0[hh], v0[hh], biases[hh], *hi[hh], None) for hh in range(heads)], tri)

    def body(s, c):
        ks, vs = kv(2 * i - 1 - s)
        qs = q_lo + q_hi
        return tuple(_sb_blocks([(qs[n], ks[n % heads], vs[n % heads], biases[n % heads], *c[n], None)
                                 for n in range(2 * heads)], tri))

    state = lax.fori_loop(0, 2 * i, body, tuple(lo) + tuple(hi))
    for n, (_, acc) in enumerate(state):
        part, hh = divmod(n, heads)
        ms = jnp.mean(acc * acc, axis=-1, keepdims=True)
        o_ref[part * blk:(part + 1) * blk, cols[hh]] = (acc * lax.rsqrt(ms + NORM_EPS) * gain_ref[hh]).astype(
            o_ref.dtype)


def _sb_prompt(proj, sb_bias, sb_gain, batch, seq, n_heads):
    blk = _tile(seq // 2, 256, 128)
    nq = seq // (2 * blk)
    heads = _tile(n_heads, 2, 1)
    groups = n_heads // heads
    width = heads * HEAD_DIM
    return pl.pallas_call(
        functools.partial(_sb_prompt_body, blk=blk, heads=heads),
        out_shape=jax.ShapeDtypeStruct((batch * seq, n_heads * HEAD_DIM), _BF16),
        grid=(batch, groups, nq),
        in_specs=[
            pl.BlockSpec(memory_space=pltpu.SMEM),
            pl.BlockSpec((2 * blk, width), lambda b, g, i: (b * nq + i, g)),
            pl.BlockSpec((seq, width), lambda b, g, i: (b, groups + g)),
            pl.BlockSpec((seq, width), lambda b, g, i: (b, 2 * groups + g)),
            pl.BlockSpec((heads, 1, HEAD_DIM), lambda b, g, i: (g, 0, 0)),
        ],
        out_specs=pl.BlockSpec((2 * blk, width), lambda b, g, i: (b * nq + i, g)),
        compiler_params=_params(("arbitrary", "arbitrary", "arbitrary")),
        name="sb_prompt",
    )(sb_bias, proj, proj, proj, sb_gain.reshape(n_heads, 1, HEAD_DIM))


_LANES = 128
_MXU_COLS = 256


def _sb_flat_blocks(q, kfs, vfs, bias, tri_ones, carry, acc, valid):
    rows, groups = q.shape[0], kfs[0].shape[0] // _LANES
    zs = [lax.dot_general(q, kf, _NT, preferred_element_type=_F32) + bias for kf in kfs]
    sps = [jnp.where(valid, _softplus2(z), 0.0) for z in zs]
    boths = []
    for sp in sps:
        parts = [sp[:, g * _LANES:(g + 1) * _LANES] for g in range(groups)]
        stacked = parts[0] if groups == 1 else jnp.concatenate(parts, axis=0)
        boths.append(jnp.dot(stacked.astype(_BF16), tri_ones, preferred_element_type=_F32))
    ws = []
    for z, sp, both in zip(zs, sps, boths):
        later = [None] * groups
        for g in reversed(range(groups)):
            later[g] = both[g * rows:(g + 1) * rows, :_LANES] + carry
            carry = carry + both[g * rows:(g + 1) * rows, _LANES:]
        later = later[0] if groups == 1 else jnp.concatenate(later, axis=1)
        ws.append(jnp.where(valid, jnp.exp2(z - sp - later), 0.0).astype(_BF16))
    for w, vf in zip(ws, vfs):
        acc = acc + jnp.dot(w, vf, preferred_element_type=_F32)
    return carry, acc


def _sb_decode_body(pt_ref, q_ref, bias_ref, tq_ref, hq_ref, hk_ref, hkn_ref, keyn_ref, knew_ref, vnew_ref,
                    gain_ref, *rest, pages_per_step):
    del pt_ref
    kv_refs = rest[:2 * pages_per_step]
    o_ref, carry_ref, acc_ref = rest[2 * pages_per_step:]
    s = pl.program_id(1)
    q = q_ref[...].astype(_BF16)
    bias = bias_ref[...]
    row = lax.broadcasted_iota(jnp.int32, (_LANES, 2 * _LANES), 0)
    col = lax.broadcasted_iota(jnp.int32, (_LANES, 2 * _LANES), 1)
    tri_ones = jnp.where(col >= _LANES, 1.0, jnp.where(row > col, 1.0, 0.0)).astype(_BF16)
    own = hq_ref[...] == hk_ref[...]

    @pl.when(s == 0)
    def _():
        never = jnp.iinfo(jnp.int32).max
        valid = jnp.where(hq_ref[...] == hkn_ref[...], keyn_ref[...], never) < tq_ref[...]
        carry, acc = _sb_flat_blocks(q, [knew_ref[...].astype(_BF16)], [vnew_ref[...].astype(_BF16)], bias,
                                     tri_ones, jnp.zeros(carry_ref.shape, _F32), jnp.zeros(acc_ref.shape, _F32),
                                     valid)
        carry_ref[...] = carry
        acc_ref[...] = acc

    def flat(ref):
        page = ref[...]
        return page.reshape(page.shape[0] * page.shape[1], page.shape[2]).astype(_BF16)

    carry, acc = _sb_flat_blocks(q, [flat(r) for r in kv_refs[0::2]], [flat(r) for r in kv_refs[1::2]], bias,
                                 tri_ones, carry_ref[...], acc_ref[...], own)
    carry_ref[...] = carry
    acc_ref[...] = acc

    @pl.when(s == pl.num_programs(1) - 1)
    def _():
        ms = jnp.mean(acc * acc, axis=-1, keepdims=True)
        o_ref[...] = acc * lax.rsqrt(ms + NORM_EPS) * gain_ref[...]


def _sb_decode(q_rows, k_new, v_new, sb_bias, sb_gain, cache_k, cache_v, layer, page_table, n_new):
    n_seq, n_rows, hd = q_rows.shape
    page, n_heads = cache_k.shape[2], cache_k.shape[3]
    n_pages = page_table.shape[1]
    pps = _tile(n_pages, 16, 1)
    n_steps = n_pages // pps
    assert n_rows <= _LANES and _LANES % n_heads == 0

    i32 = jnp.int32
    col = lambda v: v.reshape(n_rows, 1)
    bias_col = col(jnp.tile(sb_bias, n_new)) * LOG2_E
    tq_col = col(jnp.repeat(jnp.arange(n_new, dtype=i32), n_heads))
    hq_col = col(jnp.tile(jnp.arange(n_heads, dtype=i32), n_new))
    hk_row = jnp.tile(jnp.arange(n_heads, dtype=i32), page).reshape(1, page * n_heads)
    hkn_row = jnp.tile(jnp.arange(n_heads, dtype=i32), _LANES // n_heads).reshape(1, _LANES)
    keyn_row = jnp.repeat(jnp.arange(_LANES // n_heads, dtype=i32), n_heads).reshape(1, _LANES)
    gain_rows = jnp.tile(sb_gain, (n_new, 1))
    pad = lambda t: jnp.pad(t, ((0, 0), (0, _LANES - n_rows), (0, 0)))

    def page_map(r):
        return lambda b, s, pt: (layer, pt[b, n_pages - 1 - (s * pps + r)], 0, 0, 0)

    kv_specs, kv_args = [], []
    for r in range(pps):
        kv_specs += [pl.BlockSpec((None, None, page, n_heads, hd), page_map(r))] * 2
        kv_args += [cache_k, cache_v]
    seq_map = lambda b, s, pt: (b, 0, 0)
    fix_map = lambda b, s, pt: (0, 0)
    fixed = lambda a: pl.BlockSpec(a.shape, fix_map)
    small = [bias_col, tq_col, hq_col, hk_row, hkn_row, keyn_row]
    grid_spec = pltpu.PrefetchScalarGridSpec(
        num_scalar_prefetch=1,
        grid=(n_seq, n_steps),
        in_specs=[pl.BlockSpec((None, n_rows, hd), seq_map)] + [fixed(a) for a in small] + [
            pl.BlockSpec((None, _LANES, hd), seq_map),
            pl.BlockSpec((None, _LANES, hd), seq_map),
            fixed(gain_rows),
        ] + kv_specs,
        out_specs=pl.BlockSpec((None, n_rows, hd), seq_map),
        scratch_shapes=[pltpu.VMEM((n_rows, _LANES), _F32), pltpu.VMEM((n_rows, hd), _F32)],
    )
    return pl.pallas_call(
        functools.partial(_sb_decode_body, pages_per_step=pps),
        out_shape=jax.ShapeDtypeStruct((n_seq, n_rows, hd), _F32),
        grid_spec=grid_spec,
        compiler_params=_params(("arbitrary", "arbitrary")),
        name="sb_decode",
    )(page_table, q_rows, *small, pad(k_new), pad(v_new), gain_rows, *kv_args)


def _gated_group_norm(o, gain, gate):
    mu = jnp.mean(o, axis=-1, keepdims=True)
    d = o - mu
    var = jnp.mean(d * d, axis=-1, keepdims=True)
    return d * lax.rsqrt(var + NORM_EPS) * gain * (gate * jax.nn.sigmoid(gate))


def _ret_prompt_body(logg_ref, q_ref, k_ref, v_ref, g_ref, gain_ref, o_ref, s_ref, state_ref, decay_ref, *, chunk,
                     heads):
    hg = pl.program_id(1)
    c = pl.program_id(2)

    lgs = [logg_ref[hg * heads + i] for i in range(heads)]

    @pl.when(c == 0)
    def _():
        state_ref[...] = jnp.zeros_like(state_ref)
        li = lax.broadcasted_iota(jnp.int32, (chunk, chunk), 0)
        mi = lax.broadcasted_iota(jnp.int32, (chunk, chunk), 1)
        dist = (li - mi).astype(_F32)
        for i in range(heads):
            decay_ref[i] = jnp.where(dist >= 0, jnp.exp(jnp.maximum(dist, 0.0) * lgs[i]), 0.0)

    pos = lax.broadcasted_iota(jnp.int32, (chunk, 1), 0).astype(_F32)
    full = jnp.full((1, HEAD_DIM), float(chunk), _F32)
    cols = [slice(i * HEAD_DIM, (i + 1) * HEAD_DIM) for i in range(heads)]
    qs = [q_ref[:, cl].astype(_BF16) for cl in cols]
    ks = [k_ref[:, cl] for cl in cols]
    vs = [v_ref[:, cl].astype(_BF16) for cl in cols]
    states = [state_ref[i] for i in range(heads)]
    scores = [lax.dot_general(q, k.astype(_BF16), _NT, preferred_element_type=_F32) for q, k in zip(qs, ks)]
    from_state = [jnp.dot(q, s.astype(_BF16), preferred_element_type=_F32) for q, s in zip(qs, states)]
    updates = [jnp.dot((k * jnp.exp((chunk - 1.0 - pos) * lg)).T.astype(_BF16), v, preferred_element_type=_F32)
               for k, v, lg in zip(ks, vs, lgs)]
    decayed = [(sc * decay_ref[i]).astype(_BF16) for i, sc in enumerate(scores)]
    outs = [jnp.dot(d, v, preferred_element_type=_F32) for d, v in zip(decayed, vs)]
    last = c == pl.num_programs(2) - 1
    for i in range(heads):
        new_state = states[i] * jnp.exp(full * lgs[i]) + updates[i]
        state_ref[i] = new_state

        @pl.when(last)
        def _():
            s_ref[i] = new_state

        o = outs[i] + from_state[i] * jnp.exp((pos + 1.0) * lgs[i])
        o_ref[:, cols[i]] = _gated_group_norm(o, gain_ref[i], g_ref[:, cols[i]]).astype(o_ref.dtype)


def _ret_prompt(proj, log_g, ret_gain, batch, seq, n_heads):
    chunk = _tile(seq, 256, 128)
    nc = seq // chunk
    heads = _tile(n_heads, 4, 1)
    groups = n_heads // heads
    width = heads * HEAD_DIM
    col = lambda seg: (lambda b, hg, c: (b * nc + c, seg * groups + hg))
    return pl.pallas_call(
        functools.partial(_ret_prompt_body, chunk=chunk, heads=heads),
        out_shape=(jax.ShapeDtypeStruct((batch * seq, n_heads * HEAD_DIM), _BF16),
                   jax.ShapeDtypeStruct((batch, n_heads, HEAD_DIM, HEAD_DIM), _F32)),
        grid=(batch, groups, nc),
        in_specs=[
            pl.BlockSpec(memory_space=pltpu.SMEM),
            pl.BlockSpec((chunk, width), col(3)),
            pl.BlockSpec((chunk, width), col(4)),
            pl.BlockSpec((chunk, width), col(5)),
            pl.BlockSpec((chunk, width), col(6)),
            pl.BlockSpec((heads, 1, HEAD_DIM), lambda b, hg, c: (hg, 0, 0)),
        ],
        out_specs=(pl.BlockSpec((chunk, width), lambda b, hg, c: (b * nc + c, hg)),
                   pl.BlockSpec((None, heads, HEAD_DIM, HEAD_DIM), lambda b, hg, c: (b, hg, 0, 0))),
        scratch_shapes=[pltpu.VMEM((heads, HEAD_DIM, HEAD_DIM), _F32), pltpu.VMEM((heads, chunk, chunk), _F32)],
        compiler_params=_params(("arbitrary", "arbitrary", "arbitrary")),
        name="ret_prompt",
    )(log_g, proj, proj, proj, proj, ret_gain.reshape(n_heads, 1, HEAD_DIM))


def _ret_sample_body(logg_ref, q_ref, k_ref, v_ref, g_ref, gain_ref, seqc_ref, seqr_ref, posc_ref, posr_ref,
                     s0_ref, o_ref, s_ref, *, n_new):
    h = pl.program_id(0)
    lg = logg_ref[h]
    n_seq = s0_ref.shape[0]
    q = q_ref[...].astype(_BF16)
    k = k_ref[...]
    v = v_ref[...].astype(_BF16)
    seq_c, seq_r = seqc_ref[...], seqr_ref[...]
    pos_c, pos_r = posc_ref[...], posr_ref[...]
    dist = pos_c - pos_r
    decay = jnp.where(seq_c == seq_r, jnp.where(dist >= 0, jnp.exp(jnp.maximum(dist, 0.0) * lg), 0.0), 0.0)
    scores = lax.dot_general(q, k.astype(_BF16), _NT, preferred_element_type=_F32) * decay
    o = jnp.dot(scores.astype(_BF16), v, preferred_element_type=_F32)
    q_decay = jnp.exp((pos_c + 1.0) * lg)
    k_tail_t = k.T * jnp.exp((n_new - 1.0 - pos_r) * lg)
    full = jnp.full((1, HEAD_DIM), float(n_new), _F32)
    state_decay = jnp.exp(full * lg)

    def body(b, o):
        state = s0_ref[b]
        from_state = jnp.dot(q, state.astype(_BF16), preferred_element_type=_F32) * q_decay
        o = o + jnp.where(seq_c == b, from_state, 0.0)
        kt = jnp.where(seq_r == b, k_tail_t, 0.0).astype(_BF16)
        s_ref[b] = state * state_decay + jnp.dot(kt, v, preferred_element_type=_F32)
        return o

    o = lax.fori_loop(0, n_seq, body, o)
    o_ref[...] = _gated_group_norm(o, gain_ref[...], g_ref[...]).astype(o_ref.dtype)


def _ret_sample(proj_s, log_g, ret_gain, state, n_seq, n_new, n_heads):
    ts = n_seq * n_new
    seq_id = jnp.repeat(jnp.arange(n_seq, dtype=jnp.int32), n_new)
    pos = jnp.tile(jnp.arange(n_new, dtype=_F32), n_seq)
    col = lambda seg: (lambda h: (0, seg * n_heads + h))
    fix = lambda h: (0, 0)
    return pl.pallas_call(
        functools.partial(_ret_sample_body, n_new=n_new),
        out_shape=(jax.ShapeDtypeStruct((ts, n_heads * HEAD_DIM), _BF16),
                   jax.ShapeDtypeStruct((n_seq, n_heads, HEAD_DIM, HEAD_DIM), _F32)),
        grid=(n_heads,),
        in_specs=[
            pl.BlockSpec(memory_space=pltpu.SMEM),
            pl.BlockSpec((ts, HEAD_DIM), col(3)),
            pl.BlockSpec((ts, HEAD_DIM), col(4)),
            pl.BlockSpec((ts, HEAD_DIM), col(5)),
            pl.BlockSpec((ts, HEAD_DIM), col(6)),
            pl.BlockSpec((None, 1, HEAD_DIM), lambda h: (h, 0, 0)),
            pl.BlockSpec((ts, 1), fix),
            pl.BlockSpec((1, ts), fix),
            pl.BlockSpec((ts, 1), fix),
            pl.BlockSpec((1, ts), fix),
            pl.BlockSpec((n_seq, None, HEAD_DIM, HEAD_DIM), lambda h: (0, h, 0, 0)),
        ],
        out_specs=(pl.BlockSpec((ts, HEAD_DIM), lambda h: (0, h)),
                   pl.BlockSpec((n_seq, None, HEAD_DIM, HEAD_DIM), lambda h: (0, h, 0, 0))),
        compiler_params=_params(("arbitrary",)),
        name="ret_sample",
    )(log_g, proj_s, proj_s, proj_s, proj_s, ret_gain.reshape(n_heads, 1, HEAD_DIM),
      seq_id.reshape(ts, 1), seq_id.reshape(1, ts), pos.reshape(ts, 1), pos.reshape(1, ts), state)


def _outproj_body(oa_ref, ob_ref, x_ref, w_ref, g_ref, hp_ref, xn_ref):
    half = oa_ref.shape[1]
    hp = (x_ref[...]
          + jnp.dot(oa_ref[...], w_ref[:half, :], preferred_element_type=_F32)
          + jnp.dot(ob_ref[...], w_ref[half:, :], preferred_element_type=_F32))
    hp_ref[...] = hp
    ms = jnp.mean(hp * hp, axis=-1, keepdims=True)
    xn_ref[...] = (hp * lax.rsqrt(ms + NORM_EPS) * g_ref[...]).astype(xn_ref.dtype)


def _outproj(o_a, o_b, x, w_bf16, gain2):
    T, D = x.shape
    half = o_a.shape[1]
    tm = _tile(T, 320, 16)
    return pl.pallas_call(
        _outproj_body,
        out_shape=(jax.ShapeDtypeStruct((T, D), _F32), jax.ShapeDtypeStruct((T, D), _BF16)),
        grid=(T // tm,),
        in_specs=[
            pl.BlockSpec((tm, half), lambda i: (i, 0)),
            pl.BlockSpec((tm, half), lambda i: (i, 0)),
            pl.BlockSpec((tm, D), lambda i: (i, 0)),
            pl.BlockSpec((2 * half, D), lambda i: (0, 0)),
            pl.BlockSpec((1, D), lambda i: (0, 0)),
        ],
        out_specs=(pl.BlockSpec((tm, D), lambda i: (i, 0)), pl.BlockSpec((tm, D), lambda i: (i, 0))),
        compiler_params=_params(("arbitrary",)),
        name="outproj",
    )(o_a, o_b, x, w_bf16, gain2)


def _top_values(s, n):
    vals = []
    cur = s
    for _ in range(n):
        m = jnp.max(cur, axis=0, keepdims=True)
        vals.append(m)
        cur = jnp.where(cur == m, _NEG_INF, cur)
    return vals


def _stack_rows(rows):
    n = len(rows)
    ridx = lax.broadcasted_iota(jnp.int32, (n, rows[0].shape[1]), 0)
    out = jnp.broadcast_to(rows[0], ridx.shape)
    for i in range(1, n):
        out = jnp.where(ridx == i, rows[i], out)
    return out


def _router_body(xn_ref, wq_ref, sk_ref, p1_ref, e1_ref, p2_ref, e2_ref, tau_ref, q_ref):
    k = PEER_TOPK
    q_ref[...] = jnp.dot(xn_ref[...], wq_ref[...], preferred_element_type=_F32)
    tt = xn_ref.shape[0]
    ridx = lax.broadcasted_iota(jnp.int32, (k, tt), 0)

    def head(h, _):
        def scores(c):
            start = pl.multiple_of((2 * h + c) * PEER_HALF, PEER_HALF)
            qc = q_ref[:, pl.ds(start, PEER_HALF)]
            sk = sk_ref[h, c]
            q_hi = qc.astype(_BF16)
            q_lo = (qc - q_hi.astype(_F32)).astype(_BF16)
            sk_hi = sk.astype(_BF16)
            sk_lo = (sk - sk_hi.astype(_F32)).astype(_BF16)
            dot = lambda a, b: lax.dot_general(a, b, _NT, preferred_element_type=_F32)
            return dot(sk_hi, q_hi) + (dot(sk_hi, q_lo) + dot(sk_lo, q_hi))

        s1, s2 = scores(0), scores(1)
        t1, t2 = _top_values(s1, k), _top_values(s2, k)
        t1s, t2s = _stack_rows(t1), _stack_rows(t2)
        cands = [jnp.where(ridx < k // (i + 1), t1[i] + t2s, _NEG_INF) for i in range(k // 2)]
        cands.append(jnp.where(ridx >= k // 2, t1s + t2[0], _NEG_INF))
        top = t1[0] + t2[0]
        z = jnp.zeros_like(top)
        tau = top
        for _ in range(k):
            m = cands[0]
            for cnd in cands[1:]:
                m = jnp.maximum(m, cnd)
            m = jnp.max(m, axis=0, keepdims=True)
            z = z + jnp.exp(m - top)
            tau = m
            cands = [jnp.where(cnd == m, _NEG_INF, cnd) for cnd in cands]
        p1_ref[h] = jnp.where(s1 >= t1[k - 1], s1, _NEG_INF)
        p2_ref[h] = jnp.where(s2 >= t2[k - 1], s2, _NEG_INF)
        e1_ref[h] = jnp.exp(s1 - t1[0])
        e2_ref[h] = jnp.exp(s2 - t2[0]) / z
        tau_ref[h] = tau
        return 0

    def head_pair(i, _):
        head(2 * i, 0)
        return head(2 * i + 1, 0)

    lax.fori_loop(0, PEER_HEADS // 2, head_pair, 0)


def _router(xn, wq_bf16, subkeys):
    T, D = xn.shape
    tt = _tile(T, 640, 128)
    big = jax.ShapeDtypeStruct((PEER_HEADS, PEER_KEYS, T), _F32)
    big_spec = pl.BlockSpec((PEER_HEADS, PEER_KEYS, tt), lambda i: (0, 0, i))
    return pl.pallas_call(
        _router_body,
        out_shape=(big, big, big, big, jax.ShapeDtypeStruct((PEER_HEADS, 1, T), _F32)),
        grid=(T // tt,),
        in_specs=[
            pl.BlockSpec((tt, D), lambda i: (i, 0)),
            pl.BlockSpec(wq_bf16.shape, lambda i: (0, 0)),
            pl.BlockSpec(subkeys.shape, lambda i: (0, 0, 0, 0)),
        ],
        out_specs=(big_spec, big_spec, big_spec, big_spec,
                   pl.BlockSpec((PEER_HEADS, 1, tt), lambda i: (0, 0, i))),
        scratch_shapes=[pltpu.VMEM((tt, wq_bf16.shape[1]), _F32)],
        compiler_params=_params(("arbitrary",)),
        name="peer_router",
    )(xn, wq_bf16, subkeys)


def _gelu_tanh(x):
    c = 0.7978845608028654
    return x * (0.5 * (1.0 + jnp.tanh(c * (x + 0.044715 * (x * x * x)))))


def _experts_step(xn_ref, u_ref, v_ref, p1_ref, e1_ref, p2_ref, e2_ref, tau_ref, y_ref, act_ref, hid_in, hid_out,
                  first_a, a_per_tile):
    tt = y_ref.shape[0]
    for half in range(a_per_tile // 2):
        cols = slice(half * _MXU_COLS, (half + 1) * _MXU_COLS)
        for aa in range(2 * half, 2 * half + 2):
            a = first_a + aa
            acols = slice(aa * PEER_KEYS, (aa + 1) * PEER_KEYS)
            p1_rows = [p1_ref[h, pl.ds(a, 1), :] for h in range(PEER_HEADS)]
            e1_rows = [e1_ref[h, pl.ds(a, 1), :] for h in range(PEER_HEADS)]
            for tg in range(tt // _LANES):
                toks = slice(tg * _LANES, (tg + 1) * _LANES)
                gate_t = jnp.zeros((PEER_KEYS, _LANES), _F32)
                for h in range(PEER_HEADS):
                    cand = p1_rows[h][:, toks] + p2_ref[h, :, toks]
                    pair = e1_rows[h][:, toks] * e2_ref[h, :, toks]
                    gate_t = gate_t + jnp.where(cand >= tau_ref[h, :, toks], pair, 0.0)
                act_ref[toks, acols] = (_gelu_tanh(hid_in[toks, acols]) * gate_t.T).astype(act_ref.dtype)
        y_ref[...] += jnp.dot(act_ref[:, cols], v_ref[cols, :], preferred_element_type=_F32)
    hid_out[...] = lax.dot_general(xn_ref[...], u_ref[...], _NT, preferred_element_type=_F32)


def _experts_body(xn_ref, u_ref, v_ref, p1_ref, e1_ref, p2_ref, e2_ref, tau_ref, hp_ref, y_ref, act_ref, hid_a,
                  hid_b, *, a_per_tile):
    j = pl.program_id(1)
    first_a = (j - 1) * a_per_tile
    step = functools.partial(_experts_step, xn_ref, u_ref, v_ref, p1_ref, e1_ref, p2_ref, e2_ref, tau_ref, y_ref,
                             act_ref)

    @pl.when(j == 0)
    def _():
        y_ref[...] = hp_ref[...]
        hid_a[...] = lax.dot_general(xn_ref[...], u_ref[...], _NT, preferred_element_type=_F32)

    @pl.when(j % 2 == 1)
    def _():
        step(hid_a, hid_b, first_a, a_per_tile)

    @pl.when((j % 2 == 0) & (j > 0))
    def _():
        step(hid_b, hid_a, first_a, a_per_tile)


def _experts(xn, u_bf16, v_bf16, p1, e1, p2, e2, tau, hp):
    T, D = xn.shape
    n_experts = u_bf16.shape[0]
    tt = _tile(T, 640, 128)
    a_per_tile = 4
    et = a_per_tile * PEER_KEYS
    assert et == 2 * _MXU_COLS
    n_tiles = n_experts // et
    once = pl.Buffered(1)
    tok = pl.BlockSpec((PEER_HEADS, PEER_KEYS, tt), lambda i, j: (0, 0, i), pipeline_mode=once)
    return pl.pallas_call(
        functools.partial(_experts_body, a_per_tile=a_per_tile),
        out_shape=jax.ShapeDtypeStruct((T, D), _F32),
        grid=(T // tt, n_tiles + 1),
        in_specs=[
            pl.BlockSpec((tt, D), lambda i, j: (i, 0), pipeline_mode=once),
            pl.BlockSpec((et, D), lambda i, j: (jnp.minimum(j, n_tiles - 1), 0)),
            pl.BlockSpec((et, D), lambda i, j: (jnp.maximum(j - 1, 0), 0)),
            tok, tok, tok, tok,
            pl.BlockSpec((PEER_HEADS, 1, tt), lambda i, j: (0, 0, i)),
            pl.BlockSpec((tt, D), lambda i, j: (i, 0), pipeline_mode=once),
        ],
        out_specs=pl.BlockSpec((tt, D), lambda i, j: (i, 0)),
        scratch_shapes=[pltpu.VMEM((tt, et), _BF16), pltpu.VMEM((tt, et), _F32), pltpu.VMEM((tt, et), _F32)],
        compiler_params=_params(("arbitrary", "arbitrary")),
        name="peer_experts",
    )(xn, u_bf16, v_bf16, p1, e1, p2, e2, tau, hp)


def _rope_tables(pos):
    half = HEAD_DIM // 2
    freqs = ROPE_BASE ** (-jnp.arange(half, dtype=_F32) / half)
    ang = jnp.asarray(pos, _F32)[:, None] * freqs[None, :]
    cos, sin = jnp.cos(ang), jnp.sin(ang)
    return jnp.concatenate([cos, cos], axis=-1), jnp.concatenate([-sin, sin], axis=-1)


def kernel(x_prompt, x_sample, cache_k, cache_v, state_ret, page_table, norm1_gain, w_in, q_norm_gain,
           k_norm_gain, sb_bias, sb_out_gain, ret_out_gain, w_out, norm2_gain, peer_w_q, peer_subkeys,
           peer_u, peer_v):
    batch, seq, d_model = x_prompt.shape
    n_seq, n_new, _ = x_sample.shape
    depth, page, n_heads = cache_k.shape[0], cache_k.shape[2], cache_k.shape[3]
    width = n_heads * HEAD_DIM
    assert w_in.shape[2] == 7 * width and w_out.shape[1] == 2 * width
    assert n_heads * n_new % 8 == 0 and seq % (n_seq * n_new) == 0
    tp, ts = batch * seq, n_seq * n_new
    past = page_table.shape[1] * page

    pos = np.concatenate([np.tile(np.arange(seq), batch), np.tile(past + np.arange(n_new), n_seq)])
    cos_t, sin_t = _rope_tables(pos)
    log_g = jnp.log1p(-jnp.power(2.0, -5.0 - jnp.arange(n_heads, dtype=_F32)))

    h_all = jnp.concatenate([x_prompt.reshape(tp, d_model), x_sample.reshape(ts, d_model)], axis=0)
    outs = [[] for _ in range(6)]
    for l in range(depth):
        proj = _inproj(h_all, norm1_gain[l].reshape(1, d_model), w_in[l].astype(_BF16), cos_t, sin_t,
                       q_norm_gain[l], k_norm_gain[l], n_heads)
        proj_s = proj[tp:]
        oa_p = _sb_prompt(proj, sb_bias[l], sb_out_gain[l], batch, seq, n_heads)
        ob_p, ret_p = _ret_prompt(proj, log_g, ret_out_gain[l], batch, seq, n_heads)
        rows = lambda seg: proj_s[:, seg * width:(seg + 1) * width].reshape(n_seq, n_new * n_heads, HEAD_DIM)
        oa_s = _sb_decode(rows(0), rows(1), rows(2), sb_bias[l], sb_out_gain[l], cache_k, cache_v, l,
                          page_table, n_new)
        ob_s, ret_s = _ret_sample(proj_s, log_g, ret_out_gain[l], state_ret[l], n_seq, n_new, n_heads)
        o_a = jnp.concatenate([oa_p, oa_s.reshape(ts, width).astype(_BF16)], axis=0)
        o_b = jnp.concatenate([ob_p, ob_s], axis=0)
        hp, xn2 = _outproj(o_a, o_b, h_all, w_out[l].astype(_BF16), norm2_gain[l].reshape(1, d_model))
        p1, e1, p2, e2, tau = _router(xn2, peer_w_q[l].astype(_BF16), peer_subkeys[l])
        h_all = _experts(xn2, peer_u[l].astype(_BF16), peer_v[l].astype(_BF16), p1, e1, p2, e2, tau, hp)

        k_all, v_all = proj[:, width:2 * width], proj[:, 2 * width:3 * width]
        outs[0].append(k_all[:tp].reshape(batch, seq, n_heads, HEAD_DIM))
        outs[1].append(v_all[:tp].reshape(batch, seq, n_heads, HEAD_DIM))
        outs[2].append(ret_p)
        outs[3].append(k_all[tp:].reshape(n_seq, n_new, n_heads, HEAD_DIM))
        outs[4].append(v_all[tp:].reshape(n_seq, n_new, n_heads, HEAD_DIM))
        outs[5].append(ret_s)
    return (h_all[:tp].reshape(batch, seq, d_model), h_all[tp:].reshape(n_seq, n_new, d_model),
            *(jnp.stack(o) for o in outs))
```

```python
import functools

import jax
import jax.numpy as jnp
import numpy as np
from jax import lax
from jax.experimental import pallas as pl
from jax.experimental.pallas import tpu as pltpu

HEAD_DIM = 128
NORM_EPS = 1e-6
ROPE_BASE = 10000.0
PEER_HEADS = 8
PEER_KEYS = 128
PEER_TOPK = 16
PEER_HALF = 128
QK_SCALE = HEAD_DIM ** -0.5

_F32 = jnp.float32
_BF16 = jnp.bfloat16
_NEG_INF = float("-inf")
_V7X_VMEM_LIMIT_BYTES = 56 * 1024 * 1024

_NT = (((1,), (1,)), ((), ()))


def _params(sem):
    return pltpu.CompilerParams(dimension_semantics=sem, vmem_limit_bytes=_V7X_VMEM_LIMIT_BYTES)


def _tile(n, cap, mult):
    best = None
    for d in range(mult, min(n, cap) + 1, mult):
        if n % d == 0:
            best = d
    assert best is not None, (n, cap, mult)
    return best


LOG2_E = 1.4426950408889634


def _softplus2(z2):
    return jnp.maximum(z2, 0.0) + jnp.log2(1.0 + jnp.exp2(-jnp.abs(z2)))


def _strict_lower_ones(n):
    row = lax.broadcasted_iota(jnp.int32, (n, n), 0)
    col = lax.broadcasted_iota(jnp.int32, (n, n), 1)
    return jnp.where(row > col, 1.0, 0.0).astype(_BF16)


def _inproj_body(x_ref, g_ref, w_ref, cos_ref, sin_ref, qg_ref, kg_ref, o_ref, xn_ref, *, n_heads):
    j = pl.program_id(1)

    @pl.when(j == 0)
    def _():
        x = x_ref[...]
        ms = jnp.mean(x * x, axis=-1, keepdims=True)
        xn_ref[...] = (x * lax.rsqrt(ms + NORM_EPS) * g_ref[...]).astype(_BF16)

    acc = jnp.dot(xn_ref[...], w_ref[...], preferred_element_type=_F32)

    def per_head(fn):
        for h in range(n_heads):
            sl = slice(h * HEAD_DIM, (h + 1) * HEAD_DIM)
            o_ref[:, sl] = fn(acc[:, sl], h)

    def rms(a, gain):
        return a * lax.rsqrt(jnp.mean(a * a, axis=-1, keepdims=True) + NORM_EPS) * gain

    def rope(a):
        return a * cos_ref[...] + pltpu.roll(a, HEAD_DIM // 2, 1) * sin_ref[...]

    @pl.when(j == 0)
    def _():
        per_head(lambda a, h: rms(a, qg_ref[h:h + 1, :]) * (QK_SCALE * LOG2_E))

    @pl.when(j == 1)
    def _():
        per_head(lambda a, h: rms(a, kg_ref[h:h + 1, :]))

    @pl.when(j == 3)
    def _():
        per_head(lambda a, h: rope(a))

    @pl.when(j == 4)
    def _():
        per_head(lambda a, h: rope(a) * QK_SCALE)

    @pl.when((j == 2) | (j == 5) | (j == 6))
    def _():
        o_ref[...] = acc


def _inproj(x, gain, w_bf16, cos_t, sin_t, q_gain, k_gain, n_heads):
    T, D = x.shape
    width = n_heads * HEAD_DIM
    n_seg = w_bf16.shape[1] // width
    tm = _tile(T, 640, 8)
    return pl.pallas_call(
        functools.partial(_inproj_body, n_heads=n_heads),
        out_shape=jax.ShapeDtypeStruct((T, n_seg * width), _F32),
        grid=(T // tm, n_seg),
        in_specs=[
            pl.BlockSpec((tm, D), lambda i, j: (i, 0)),
            pl.BlockSpec((1, D), lambda i, j: (0, 0)),
            pl.BlockSpec((D, width), lambda i, j: (0, j)),
            pl.BlockSpec((tm, HEAD_DIM), lambda i, j: (i, 0)),
            pl.BlockSpec((tm, HEAD_DIM), lambda i, j: (i, 0)),
            pl.BlockSpec((n_heads, HEAD_DIM), lambda i, j: (0, 0)),
            pl.BlockSpec((n_heads, HEAD_DIM), lambda i, j: (0, 0)),
        ],
        out_specs=pl.BlockSpec((tm, width), lambda i, j: (i, j)),
        scratch_shapes=[pltpu.VMEM((tm, D), _BF16)],
        compiler_params=_params(("arbitrary", "arbitrary")),
        name="inproj",
    )(x, gain, w_bf16, cos_t, sin_t, q_gain, k_gain)


def _sb_blocks(chains, tri):
    masked = lambda x, mask: x if mask is None else jnp.where(mask, x, 0.0)
    zs = [lax.dot_general(c[0], c[1], _NT, preferred_element_type=_F32) + c[3] for c in chains]
    sps = [masked(_softplus2(z), c[6]) for z, c in zip(zs, chains)]
    laters = [jnp.dot(sp.astype(_BF16), tri, preferred_element_type=_F32) for sp in sps]
    out = []
    for z, sp, later, (_, _, vs, _, carry, acc, mask) in zip(zs, sps, laters, chains):
        w = masked(jnp.exp2(z - sp - (carry + later)), mask)
        acc = acc + jnp.dot(w.astype(_BF16), vs, preferred_element_type=_F32)
        out.append((carry + jnp.sum(sp, axis=1, keepdims=True), acc))
    return out


def _sb_prompt_body(bias_ref, q_ref, k_ref, v_ref, gain_ref, o_ref, *, blk, heads):
    hg = pl.program_id(1)
    i = pl.program_id(2)
    tri = _strict_lower_ones(blk)
    row = lax.broadcasted_iota(jnp.int32, (blk, blk), 0)
    col = lax.broadcasted_iota(jnp.int32, (blk, blk), 1)
    causal = col < row
    cols = [slice(hh * HEAD_DIM, (hh + 1) * HEAD_DIM) for hh in range(heads)]
    biases = [bias_ref[hg * heads + hh] * LOG2_E for hh in range(heads)]
    q_lo = [q_ref[:blk, cl].astype(_BF16) for cl in cols]
    q_hi = [q_ref[blk:, cl].astype(_BF16) for cl in cols]

    def kv(kb):
        start = pl.multiple_of(kb * blk, blk)
        return ([k_ref[pl.ds(start, blk), cl].astype(_BF16) for cl in cols],
                [v_ref[pl.ds(start, blk), cl].astype(_BF16) for cl in cols])

    zero = (jnp.zeros((blk, 1), _F32), jnp.zeros((blk, HEAD_DIM), _F32))
    k1, v1 = kv(2 * i + 1)
    k0, v0 = kv(2 * i)
    state = _sb_blocks([(q_lo[hh], k0[hh], v0[hh], biases[hh], *zero, causal) for hh in range(heads)]
                       + [(q_hi[hh], k1[hh], v1[hh], biases[hh], *zero, causal) for hh in range(heads)], tri)
    lo, hi = state[:heads], state[heads:]
    hi = _sb_blocks([(q_hi[hh], k0[hh], v0[hh], biases[hh], *hi[hh], None) for hh in range(heads)], tri)

    def body(s, c):
        ks, vs = kv(2 * i - 1 - s)
        qs = q_lo + q_hi
        return tuple(_sb_blocks([(qs[n], ks[n % heads], vs[n % heads], biases[n % heads], *c[n], None)
                                 for n in range(2 * heads)], tri))

    state = lax.fori_loop(0, 2 * i, body, tuple(lo) + tuple(hi))
    for n, (_, acc) in enumerate(state):
        part, hh = divmod(n, heads)
        ms = jnp.mean(acc * acc, axis=-1, keepdims=True)
        o_ref[part * blk:(part + 1) * blk, cols[hh]] = (acc * lax.rsqrt(ms + NORM_EPS) * gain_ref[hh]).astype(
            o_ref.dtype)


def _sb_prompt(proj, sb_bias, sb_gain, batch, seq, n_heads):
    blk = _tile(seq // 2, 256, 128)
    nq = seq // (2 * blk)
    heads = _tile(n_heads, 2, 1)
    groups = n_heads // heads
    width = heads * HEAD_DIM
    return pl.pallas_call(
        functools.partial(_sb_prompt_body, blk=blk, heads=heads),
        out_shape=jax.ShapeDtypeStruct((batch * seq, n_heads * HEAD_DIM), _BF16),
        grid=(batch, groups, nq),
        in_specs=[
            pl.BlockSpec(memory_space=pltpu.SMEM),
            pl.BlockSpec((2 * blk, width), lambda b, g, i: (b * nq + i, g)),
            pl.BlockSpec((seq, width), lambda b, g, i: (b, groups + g)),
            pl.BlockSpec((seq, width), lambda b, g, i: (b, 2 * groups + g)),
            pl.BlockSpec((heads, 1, HEAD_DIM), lambda b, g, i: (g, 0, 0)),
        ],
        out_specs=pl.BlockSpec((2 * blk, width), lambda b, g, i: (b * nq + i, g)),
        compiler_params=_params(("arbitrary", "arbitrary", "arbitrary")),
        name="sb_prompt",
    )(sb_bias, proj, proj, proj, sb_gain.reshape(n_heads, 1, HEAD_DIM))


_LANES = 128
_MXU_COLS = 256


def _sb_flat_blocks(q, kfs, vfs, bias, tri_ones, carry, acc, valid):
    rows, groups = q.shape[0], kfs[0].shape[0] // _LANES
    zs = [lax.dot_general(q, kf, _NT, preferred_element_type=_F32) + bias for kf in kfs]
    sps = [jnp.where(valid, _softplus2(z), 0.0) for z in zs]
    boths = []
    for sp in sps:
        parts = [sp[:, g * _LANES:(g + 1) * _LANES] for g in range(groups)]
        stacked = parts[0] if groups == 1 else jnp.concatenate(parts, axis=0)
        boths.append(jnp.dot(stacked.astype(_BF16), tri_ones, preferred_element_type=_F32))
    ws = []
    for z, sp, both in zip(zs, sps, boths):
        later = [None] * groups
        for g in reversed(range(groups)):
            later[g] = both[g * rows:(g + 1) * rows, :_LANES] + carry
            carry = carry + both[g * rows:(g + 1) * rows, _LANES:]
        later = later[0] if groups == 1 else jnp.concatenate(later, axis=1)
        ws.append(jnp.where(valid, jnp.exp2(z - sp - later), 0.0).astype(_BF16))
    for w, vf in zip(ws, vfs):
        acc = acc + jnp.dot(w, vf, preferred_element_type=_F32)
    return carry, acc


def _sb_decode_body(pt_ref, q_ref, bias_ref, tq_ref, hq_ref, hk_ref, hkn_ref, keyn_ref, knew_ref, vnew_ref,
                    gain_ref, *rest, pages_per_step):
    del pt_ref
    kv_refs = rest[:2 * pages_per_step]
    o_ref, carry_ref, acc_ref = rest[2 * pages_per_step:]
    s = pl.program_id(1)
    q = q_ref[...].astype(_BF16)
    bias = bias_ref[...]
    row = lax.broadcasted_iota(jnp.int32, (_LANES, 2 * _LANES), 0)
    col = lax.broadcasted_iota(jnp.int32, (_LANES, 2 * _LANES), 1)
    tri_ones = jnp.where(col >= _LANES, 1.0, jnp.where(row > col, 1.0, 0.0)).astype(_BF16)
    own = hq_ref[...] == hk_ref[...]

    @pl.when(s == 0)
    def _():
        never = jnp.iinfo(jnp.int32).max
        valid = jnp.where(hq_ref[...] == hkn_ref[...], keyn_ref[...], never) < tq_ref[...]
        carry, acc = _sb_flat_blocks(q, [knew_ref[...].astype(_BF16)], [vnew_ref[...].astype(_BF16)], bias,
                                     tri_ones, jnp.zeros(carry_ref.shape, _F32), jnp.zeros(acc_ref.shape, _F32),
                                     valid)
        carry_ref[...] = carry
        acc_ref[...] = acc

    def flat(ref):
        page = ref[...]
        return page.reshape(page.shape[0] * page.shape[1], page.shape[2]).astype(_BF16)

    carry, acc = _sb_flat_blocks(q, [flat(r) for r in kv_refs[0::2]], [flat(r) for r in kv_refs[1::2]], bias,
                                 tri_ones, carry_ref[...], acc_ref[...], own)
    carry_ref[...] = carry
    acc_ref[...] = acc

    @pl.when(s == pl.num_programs(1) - 1)
    def _():
        ms = jnp.mean(acc * acc, axis=-1, keepdims=True)
        o_ref[...] = acc * lax.rsqrt(ms + NORM_EPS) * gain_ref[...]


def _sb_decode(q_rows, k_new, v_new, sb_bias, sb_gain, cache_k, cache_v, layer, page_table, n_new):
    n_seq, n_rows, hd = q_rows.shape
    page, n_heads = cache_k.shape[2], cache_k.shape[3]
    n_pages = page_table.shape[1]
    pps = _tile(n_pages, 16, 1)
    n_steps = n_pages // pps
    assert n_rows <= _LANES and _LANES % n_heads == 0

    i32 = jnp.int32
    col = lambda v: v.reshape(n_rows, 1)
    bias_col = col(jnp.tile(sb_bias, n_new)) * LOG2_E
    tq_col = col(jnp.repeat(jnp.arange(n_new, dtype=i32), n_heads))
    hq_col = col(jnp.tile(jnp.arange(n_heads, dtype=i32), n_new))
    hk_row = jnp.tile(jnp.arange(n_heads, dtype=i32), page).reshape(1, page * n_heads)
    hkn_row = jnp.tile(jnp.arange(n_heads, dtype=i32), _LANES // n_heads).reshape(1, _LANES)
    keyn_row = jnp.repeat(jnp.arange(_LANES // n_heads, dtype=i32), n_heads).reshape(1, _LANES)
    gain_rows = jnp.tile(sb_gain, (n_new, 1))
    pad = lambda t: jnp.pad(t, ((0, 0), (0, _LANES - n_rows), (0, 0)))

    def page_map(r):
        return lambda b, s, pt: (layer, pt[b, n_pages - 1 - (s * pps + r)], 0, 0, 0)

    kv_specs, kv_args = [], []
    for r in range(pps):
        kv_specs += [pl.BlockSpec((None, None, page, n_heads, hd), page_map(r))] * 2
        kv_args += [cache_k, cache_v]
    seq_map = lambda b, s, pt: (b, 0, 0)
    fix_map = lambda b, s, pt: (0, 0)
    fixed = lambda a: pl.BlockSpec(a.shape, fix_map)
    small = [bias_col, tq_col, hq_col, hk_row, hkn_row, keyn_row]
    grid_spec = pltpu.PrefetchScalarGridSpec(
        num_scalar_prefetch=1,
        grid=(n_seq, n_steps),
        in_specs=[pl.BlockSpec((None, n_rows, hd), seq_map)] + [fixed(a) for a in small] + [
            pl.BlockSpec((None, _LANES, hd), seq_map),
            pl.BlockSpec((None, _LANES, hd), seq_map),
            fixed(gain_rows),
        ] + kv_specs,
        out_specs=pl.BlockSpec((None, n_rows, hd), seq_map),
        scratch_shapes=[pltpu.VMEM((n_rows, _LANES), _F32), pltpu.VMEM((n_rows, hd), _F32)],
    )
    return pl.pallas_call(
        functools.partial(_sb_decode_body, pages_per_step=pps),
        out_shape=jax.ShapeDtypeStruct((n_seq, n_rows, hd), _F32),
        grid_spec=grid_spec,
        compiler_params=_params(("arbitrary", "arbitrary")),
        name="sb_decode",
    )(page_table, q_rows, *small, pad(k_new), pad(v_new), gain_rows, *kv_args)


def _gated_group_norm(o, gain, gate):
    mu = jnp.mean(o, axis=-1, keepdims=True)
    d = o - mu
    var = jnp.mean(d * d, axis=-1, keepdims=True)
    return d * lax.rsqrt(var + NORM_EPS) * gain * (gate * jax.nn.sigmoid(gate))


def _ret_prompt_body(logg_ref, q_ref, k_ref, v_ref, g_ref, gain_ref, o_ref, s_ref, state_ref, decay_ref, *, chunk,
                     heads):
    hg = pl.program_id(1)
    c = pl.program_id(2)

    lgs = [logg_ref[hg * heads + i] for i in range(heads)]

    @pl.when(c == 0)
    def _():
        state_ref[...] = jnp.zeros_like(state_ref)
        li = lax.broadcasted_iota(jnp.int32, (chunk, chunk), 0)
        mi = lax.broadcasted_iota(jnp.int32, (chunk, chunk), 1)
        dist = (li - mi).astype(_F32)
        for i in range(heads):
            decay_ref[i] = jnp.where(dist >= 0, jnp.exp(jnp.maximum(dist, 0.0) * lgs[i]), 0.0)

    pos = lax.broadcasted_iota(jnp.int32, (chunk, 1), 0).astype(_F32)
    full = jnp.full((1, HEAD_DIM), float(chunk), _F32)
    cols = [slice(i * HEAD_DIM, (i + 1) * HEAD_DIM) for i in range(heads)]
    qs = [q_ref[:, cl].astype(_BF16) for cl in cols]
    ks = [k_ref[:, cl] for cl in cols]
    vs = [v_ref[:, cl].astype(_BF16) for cl in cols]
    states = [state_ref[i] for i in range(heads)]
    scores = [lax.dot_general(q, k.astype(_BF16), _NT, preferred_element_type=_F32) for q, k in zip(qs, ks)]
    from_state = [jnp.dot(q, s.astype(_BF16), preferred_element_type=_F32) for q, s in zip(qs, states)]
    updates = [jnp.dot((k * jnp.exp((chunk - 1.0 - pos) * lg)).T.astype(_BF16), v, preferred_element_type=_F32)
               for k, v, lg in zip(ks, vs, lgs)]
    decayed = [(sc * decay_ref[i]).astype(_BF16) for i, sc in enumerate(scores)]
    outs = [jnp.dot(d, v, preferred_element_type=_F32) for d, v in zip(decayed, vs)]
    last = c == pl.num_programs(2) - 1
    for i in range(heads):
        new_state = states[i] * jnp.exp(full * lgs[i]) + updates[i]
        state_ref[i] = new_state

        @pl.when(last)
        def _():
            s_ref[i] = new_state

        o = outs[i] + from_state[i] * jnp.exp((pos + 1.0) * lgs[i])
        o_ref[:, cols[i]] = _gated_group_norm(o, gain_ref[i], g_ref[:, cols[i]]).astype(o_ref.dtype)


def _ret_prompt(proj, log_g, ret_gain, batch, seq, n_heads):
    chunk = _tile(seq, 256, 128)
    nc = seq // chunk
    heads = _tile(n_heads, 4, 1)
    groups = n_heads // heads
    width = heads * HEAD_DIM
    col = lambda seg: (lambda b, hg, c: (b * nc + c, seg * groups + hg))
    return pl.pallas_call(
        functools.partial(_ret_prompt_body, chunk=chunk, heads=heads),
        out_shape=(jax.ShapeDtypeStruct((batch * seq, n_heads * HEAD_DIM), _BF16),
                   jax.ShapeDtypeStruct((batch, n_heads, HEAD_DIM, HEAD_DIM), _F32)),
        grid=(batch, groups, nc),
        in_specs=[
            pl.BlockSpec(memory_space=pltpu.SMEM),
            pl.BlockSpec((chunk, width), col(3)),
            pl.BlockSpec((chunk, width), col(4)),
            pl.BlockSpec((chunk, width), col(5)),
            pl.BlockSpec((chunk, width), col(6)),
            pl.BlockSpec((heads, 1, HEAD_DIM), lambda b, hg, c: (hg, 0, 0)),
        ],
        out_specs=(pl.BlockSpec((chunk, width), lambda b, hg, c: (b * nc + c, hg)),
                   pl.BlockSpec((None, heads, HEAD_DIM, HEAD_DIM), lambda b, hg, c: (b, hg, 0, 0))),
        scratch_shapes=[pltpu.VMEM((heads, HEAD_DIM, HEAD_DIM), _F32), pltpu.VMEM((heads, chunk, chunk), _F32)],
        compiler_params=_params(("arbitrary", "arbitrary", "arbitrary")),
        name="ret_prompt",
    )(log_g, proj, proj, proj, proj, ret_gain.reshape(n_heads, 1, HEAD_DIM))


def _ret_sample_body(logg_ref, q_ref, k_ref, v_ref, g_ref, gain_ref, seqc_ref, seqr_ref, posc_ref, posr_ref,
                     s0_ref, o_ref, s_ref, *, n_new):
    h = pl.program_id(0)
    lg = logg_ref[h]
    n_seq = s0_ref.shape[0]
    q = q_ref[...].astype(_BF16)
    k = k_ref[...]
    v = v_ref[...].astype(_BF16)
    seq_c, seq_r = seqc_ref[...], seqr_ref[...]
    pos_c, pos_r = posc_ref[...], posr_ref[...]
    dist = pos_c - pos_r
    decay = jnp.where(seq_c == seq_r, jnp.where(dist >= 0, jnp.exp(jnp.maximum(dist, 0.0) * lg), 0.0), 0.0)
    scores = lax.dot_general(q, k.astype(_BF16), _NT, preferred_element_type=_F32) * decay
    o = jnp.dot(scores.astype(_BF16), v, preferred_element_type=_F32)
    q_decay = jnp.exp((pos_c + 1.0) * lg)
    k_tail_t = k.T * jnp.exp((n_new - 1.0 - pos_r) * lg)
    full = jnp.full((1, HEAD_DIM), float(n_new), _F32)
    state_decay = jnp.exp(full * lg)

    def body(b, o):
        state = s0_ref[b]
        from_state = jnp.dot(q, state.astype(_BF16), preferred_element_type=_F32) * q_decay
        o = o + jnp.where(seq_c == b, from_state, 0.0)
        kt = jnp.where(seq_r == b, k_tail_t, 0.0).astype(_BF16)
        s_ref[b] = state * state_decay + jnp.dot(kt, v, preferred_element_type=_F32)
        return o

    o = lax.fori_loop(0, n_seq, body, o)
    o_ref[...] = _gated_group_norm(o, gain_ref[...], g_ref[...]).astype(o_ref.dtype)


def _ret_sample(proj_s, log_g, ret_gain, state, n_seq, n_new, n_heads):
    ts = n_seq * n_new
    seq_id = jnp.repeat(jnp.arange(n_seq, dtype=jnp.int32), n_new)
    pos = jnp.tile(jnp.arange(n_new, dtype=_F32), n_seq)
    col = lambda seg: (lambda h: (0, seg * n_heads + h))
    fix = lambda h: (0, 0)
    return pl.pallas_call(
        functools.partial(_ret_sample_body, n_new=n_new),
        out_shape=(jax.ShapeDtypeStruct((ts, n_heads * HEAD_DIM), _BF16),
                   jax.ShapeDtypeStruct((n_seq, n_heads, HEAD_DIM, HEAD_DIM), _F32)),
        grid=(n_heads,),
        in_specs=[
            pl.BlockSpec(memory_space=pltpu.SMEM),
            pl.BlockSpec((ts, HEAD_DIM), col(3)),
            pl.BlockSpec((ts, HEAD_DIM), col(4)),
            pl.BlockSpec((ts, HEAD_DIM), col(5)),
            pl.BlockSpec((ts, HEAD_DIM), col(6)),
            pl.BlockSpec((None, 1, HEAD_DIM), lambda h: (h, 0, 0)),
            pl.BlockSpec((ts, 1), fix),
            pl.BlockSpec((1, ts), fix),
            pl.BlockSpec((ts, 1), fix),
            pl.BlockSpec((1, ts), fix),
            pl.BlockSpec((n_seq, None, HEAD_DIM, HEAD_DIM), lambda h: (0, h, 0, 0)),
        ],
        out_specs=(pl.BlockSpec((ts, HEAD_DIM), lambda h: (0, h)),
                   pl.BlockSpec((n_seq, None, HEAD_DIM, HEAD_DIM), lambda h: (0, h, 0, 0))),
        compiler_params=_params(("arbitrary",)),
        name="ret_sample",
    )(log_g, proj_s, proj_s, proj_s, proj_s, ret_gain.reshape(n_heads, 1, HEAD_DIM),
      seq_id.reshape(ts, 1), seq_id.reshape(1, ts), pos.reshape(ts, 1), pos.reshape(1, ts), state)


def _outproj_body(oa_ref, ob_ref, x_ref, w_ref, g_ref, hp_ref, xn_ref):
    half = oa_ref.shape[1]
    hp = (x_ref[...]
          + jnp.dot(oa_ref[...], w_ref[:half, :], preferred_element_type=_F32)
          + jnp.dot(ob_ref[...], w_ref[half:, :], preferred_element_type=_F32))
    hp_ref[...] = hp
    ms = jnp.mean(hp * hp, axis=-1, keepdims=True)
    xn_ref[...] = (hp * lax.rsqrt(ms + NORM_EPS) * g_ref[...]).astype(xn_ref.dtype)


def _outproj(o_a, o_b, x, w_bf16, gain2):
    T, D = x.shape
    half = o_a.shape[1]
    tm = _tile(T, 320, 16)
    return pl.pallas_call(
        _outproj_body,
        out_shape=(jax.ShapeDtypeStruct((T, D), _F32), jax.ShapeDtypeStruct((T, D), _BF16)),
        grid=(T // tm,),
        in_specs=[
            pl.BlockSpec((tm, half), lambda i: (i, 0)),
            pl.BlockSpec((tm, half), lambda i: (i, 0)),
            pl.BlockSpec((tm, D), lambda i: (i, 0)),
            pl.BlockSpec((2 * half, D), lambda i: (0, 0)),
            pl.BlockSpec((1, D), lambda i: (0, 0)),
        ],
        out_specs=(pl.BlockSpec((tm, D), lambda i: (i, 0)), pl.BlockSpec((tm, D), lambda i: (i, 0))),
        compiler_params=_params(("arbitrary",)),
        name="outproj",
    )(o_a, o_b, x, w_bf16, gain2)


_SUBLANES = 8


def _merge_sort_network(lo, hi):
    def merge(lo, hi, r):
        step = 2 * r
        if step < hi - lo:
            yield from merge(lo, hi, step)
            yield from merge(lo + r, hi, step)
            yield from ((i, i + r) for i in range(lo + r, hi - r, step))
        else:
            yield (lo, lo + r)

    if hi > lo:
        mid = lo + (hi - lo) // 2
        yield from _merge_sort_network(lo, mid)
        yield from _merge_sort_network(mid + 1, hi)
        yield from merge(lo, hi, 1)


def _top_values(s, n):
    assert s.shape[0] == n * _SUBLANES and n & (n - 1) == 0

    def exchange(v, i, j):
        v[i], v[j] = jnp.maximum(v[i], v[j]), jnp.minimum(v[i], v[j])

    slabs = [s[i * _SUBLANES:(i + 1) * _SUBLANES, :] for i in range(n)]
    for i, j in _merge_sort_network(0, n - 1):
        exchange(slabs, i, j)
    shift = _SUBLANES // 2
    while shift:
        slabs = [jnp.maximum(slabs[k], pltpu.roll(slabs[n - 1 - k], shift, 0)) for k in range(n)]
        d = n // 2
        while d:
            for k in range(n):
                if k & d == 0:
                    exchange(slabs, k, k + d)
            d //= 2
        shift //= 2
    return [v[0:1, :] for v in slabs]


def _stack_rows(rows):
    n = len(rows)
    ridx = lax.broadcasted_iota(jnp.int32, (n, rows[0].shape[1]), 0)
    out = jnp.broadcast_to(rows[0], ridx.shape)
    for i in range(1, n):
        out = jnp.where(ridx == i, rows[i], out)
    return out


def _router_body(xn_ref, wq_ref, sk_ref, p1_ref, e1_ref, p2_ref, e2_ref, tau_ref, q_ref):
    k = PEER_TOPK
    q_ref[...] = jnp.dot(xn_ref[...], wq_ref[...], preferred_element_type=_F32)
    tt = xn_ref.shape[0]
    ridx = lax.broadcasted_iota(jnp.int32, (k, tt), 0)

    def head(h, _):
        def scores(c):
            start = pl.multiple_of((2 * h + c) * PEER_HALF, PEER_HALF)
            qc = q_ref[:, pl.ds(start, PEER_HALF)]
            sk = sk_ref[h, c]
            q_hi = qc.astype(_BF16)
            q_lo = (qc - q_hi.astype(_F32)).astype(_BF16)
            sk_hi = sk.astype(_BF16)
            sk_lo = (sk - sk_hi.astype(_F32)).astype(_BF16)
            dot = lambda a, b: lax.dot_general(a, b, _NT, preferred_element_type=_F32)
            return dot(sk_hi, q_hi) + (dot(sk_hi, q_lo) + dot(sk_lo, q_hi))

        s1, s2 = scores(0), scores(1)
        t1, t2 = _top_values(s1, k), _top_values(s2, k)
        t1s, t2s = _stack_rows(t1), _stack_rows(t2)
        cands = [jnp.where(ridx < k // (i + 1), t1[i] + t2s, _NEG_INF) for i in range(k // 2)]
        cands.append(jnp.where(ridx >= k // 2, t1s + t2[0], _NEG_INF))
        top = t1[0] + t2[0]
        z = jnp.zeros_like(top)
        tau = top
        for _ in range(k):
            m = cands[0]
            for cnd in cands[1:]:
                m = jnp.maximum(m, cnd)
            m = jnp.max(m, axis=0, keepdims=True)
            z = z + jnp.exp(m - top)
            tau = m
            cands = [jnp.where(cnd == m, _NEG_INF, cnd) for cnd in cands]
        p1_ref[h] = jnp.where(s1 >= t1[k - 1], s1, _NEG_INF)
        p2_ref[h] = jnp.where(s2 >= t2[k - 1], s2, _NEG_INF)
        e1_ref[h] = jnp.exp(s1 - t1[0])
        e2_ref[h] = jnp.exp(s2 - t2[0]) / z
        tau_ref[h] = tau
        return 0

    def head_pair(i, _):
        head(2 * i, 0)
        return head(2 * i + 1, 0)

    lax.fori_loop(0, PEER_HEADS // 2, head_pair, 0)


def _router(xn, wq_bf16, subkeys):
    T, D = xn.shape
    tt = _tile(T, 640, 128)
    big = jax.ShapeDtypeStruct((PEER_HEADS, PEER_KEYS, T), _F32)
    big_spec = pl.BlockSpec((PEER_HEADS, PEER_KEYS, tt), lambda i: (0, 0, i))
    return pl.pallas_call(
        _router_body,
        out_shape=(big, big, big, big, jax.ShapeDtypeStruct((PEER_HEADS, 1, T), _F32)),
        grid=(T // tt,),
        in_specs=[
            pl.BlockSpec((tt, D), lambda i: (i, 0)),
            pl.BlockSpec(wq_bf16.shape, lambda i: (0, 0)),
            pl.BlockSpec(subkeys.shape, lambda i: (0, 0, 0, 0)),
        ],
        out_specs=(big_spec, big_spec, big_spec, big_spec,
                   pl.BlockSpec((PEER_HEADS, 1, tt), lambda i: (0, 0, i))),
        scratch_shapes=[pltpu.VMEM((tt, wq_bf16.shape[1]), _F32)],
        compiler_params=_params(("arbitrary",)),
        name="peer_router",
    )(xn, wq_bf16, subkeys)


def _gelu_tanh(x):
    c = 0.7978845608028654
    return x * (0.5 * (1.0 + jnp.tanh(c * (x + 0.044715 * (x * x * x)))))


def _experts_step(xn_ref, u_ref, v_ref, p1_ref, e1_ref, p2_ref, e2_ref, tau_ref, y_ref, act_ref, hid_in, hid_out,
                  first_a, a_per_tile):
    tt = y_ref.shape[0]
    for half in range(a_per_tile // 2):
        cols = slice(half * _MXU_COLS, (half + 1) * _MXU_COLS)
        for aa in range(2 * half, 2 * half + 2):
            a = first_a + aa
            acols = slice(aa * PEER_KEYS, (aa + 1) * PEER_KEYS)
            p1_rows = [p1_ref[h, pl.ds(a, 1), :] for h in range(PEER_HEADS)]
            e1_rows = [e1_ref[h, pl.ds(a, 1), :] for h in range(PEER_HEADS)]
            for tg in range(tt // _LANES):
                toks = slice(tg * _LANES, (tg + 1) * _LANES)
                gate_t = jnp.zeros((PEER_KEYS, _LANES), _F32)
                for h in range(PEER_HEADS):
                    cand = p1_rows[h][:, toks] + p2_ref[h, :, toks]
                    pair = e1_rows[h][:, toks] * e2_ref[h, :, toks]
                    gate_t = gate_t + jnp.where(cand >= tau_ref[h, :, toks], pair, 0.0)
                act_ref[toks, acols] = (_gelu_tanh(hid_in[toks, acols]) * gate_t.T).astype(act_ref.dtype)
        y_ref[...] += jnp.dot(act_ref[:, cols], v_ref[cols, :], preferred_element_type=_F32)
    hid_out[...] = lax.dot_general(xn_ref[...], u_ref[...], _NT, preferred_element_type=_F32)


def _experts_body(xn_ref, u_ref, v_ref, p1_ref, e1_ref, p2_ref, e2_ref, tau_ref, hp_ref, y_ref, act_ref, hid_a,
                  hid_b, *, a_per_tile):
    j = pl.program_id(1)
    first_a = (j - 1) * a_per_tile
    step = functools.partial(_experts_step, xn_ref, u_ref, v_ref, p1_ref, e1_ref, p2_ref, e2_ref, tau_ref, y_ref,
                             act_ref)

    @pl.when(j == 0)
    def _():
        y_ref[...] = hp_ref[...]
        hid_a[...] = lax.dot_general(xn_ref[...], u_ref[...], _NT, preferred_element_type=_F32)

    @pl.when(j % 2 == 1)
    def _():
        step(hid_a, hid_b, first_a, a_per_tile)

    @pl.when((j % 2 == 0) & (j > 0))
    def _():
        step(hid_b, hid_a, first_a, a_per_tile)


def _experts(xn, u_bf16, v_bf16, p1, e1, p2, e2, tau, hp):
    T, D = xn.shape
    n_experts = u_bf16.shape[0]
    tt = _tile(T, 640, 128)
    a_per_tile = 4
    et = a_per_tile * PEER_KEYS
    assert et == 2 * _MXU_COLS
    n_tiles = n_experts // et
    once = pl.Buffered(1)
    tok = pl.BlockSpec((PEER_HEADS, PEER_KEYS, tt), lambda i, j: (0, 0, i), pipeline_mode=once)
    return pl.pallas_call(
        functools.partial(_experts_body, a_per_tile=a_per_tile),
        out_shape=jax.ShapeDtypeStruct((T, D), _F32),
        grid=(T // tt, n_tiles + 1),
        in_specs=[
            pl.BlockSpec((tt, D), lambda i, j: (i, 0), pipeline_mode=once),
            pl.BlockSpec((et, D), lambda i, j: (jnp.minimum(j, n_tiles - 1), 0)),
            pl.BlockSpec((et, D), lambda i, j: (jnp.maximum(j - 1, 0), 0)),
            tok, tok, tok, tok,
            pl.BlockSpec((PEER_HEADS, 1, tt), lambda i, j: (0, 0, i)),
            pl.BlockSpec((tt, D), lambda i, j: (i, 0), pipeline_mode=once),
        ],
        out_specs=pl.BlockSpec((tt, D), lambda i, j: (i, 0)),
        scratch_shapes=[pltpu.VMEM((tt, et), _BF16), pltpu.VMEM((tt, et), _F32), pltpu.VMEM((tt, et), _F32)],
        compiler_params=_params(("arbitrary", "arbitrary")),
        name="peer_experts",
    )(xn, u_bf16, v_bf16, p1, e1, p2, e2, tau, hp)


def _rope_tables(pos):
    half = HEAD_DIM // 2
    freqs = ROPE_BASE ** (-jnp.arange(half, dtype=_F32) / half)
    ang = jnp.asarray(pos, _F32)[:, None] * freqs[None, :]
    cos, sin = jnp.cos(ang), jnp.sin(ang)
    return jnp.concatenate([cos, cos], axis=-1), jnp.concatenate([-sin, sin], axis=-1)


def kernel(x_prompt, x_sample, cache_k, cache_v, state_ret, page_table, norm1_gain, w_in, q_norm_gain,
           k_norm_gain, sb_bias, sb_out_gain, ret_out_gain, w_out, norm2_gain, peer_w_q, peer_subkeys,
           peer_u, peer_v):
    batch, seq, d_model = x_prompt.shape
    n_seq, n_new, _ = x_sample.shape
    depth, page, n_heads = cache_k.shape[0], cache_k.shape[2], cache_k.shape[3]
    width = n_heads * HEAD_DIM
    assert w_in.shape[2] == 7 * width and w_out.shape[1] == 2 * width
    assert n_heads * n_new % 8 == 0 and seq % (n_seq * n_new) == 0
    tp, ts = batch * seq, n_seq * n_new
    past = page_table.shape[1] * page

    pos = np.concatenate([np.tile(np.arange(seq), batch), np.tile(past + np.arange(n_new), n_seq)])
    cos_t, sin_t = _rope_tables(pos)
    log_g = jnp.log1p(-jnp.power(2.0, -5.0 - jnp.arange(n_heads, dtype=_F32)))

    h_all = jnp.concatenate([x_prompt.reshape(tp, d_model), x_sample.reshape(ts, d_model)], axis=0)
    outs = [[] for _ in range(6)]
    for l in range(depth):
        proj = _inproj(h_all, norm1_gain[l].reshape(1, d_model), w_in[l].astype(_BF16), cos_t, sin_t,
                       q_norm_gain[l], k_norm_gain[l], n_heads)
        proj_s = proj[tp:]
        oa_p = _sb_prompt(proj, sb_bias[l], sb_out_gain[l], batch, seq, n_heads)
        ob_p, ret_p = _ret_prompt(proj, log_g, ret_out_gain[l], batch, seq, n_heads)
        rows = lambda seg: proj_s[:, seg * width:(seg + 1) * width].reshape(n_seq, n_new * n_heads, HEAD_DIM)
        oa_s = _sb_decode(rows(0), rows(1), rows(2), sb_bias[l], sb_out_gain[l], cache_k, cache_v, l,
                          page_table, n_new)
        ob_s, ret_s = _ret_sample(proj_s, log_g, ret_out_gain[l], state_ret[l], n_seq, n_new, n_heads)
        o_a = jnp.concatenate([oa_p, oa_s.reshape(ts, width).astype(_BF16)], axis=0)
        o_b = jnp.concatenate([ob_p, ob_s], axis=0)
        hp, xn2 = _outproj(o_a, o_b, h_all, w_out[l].astype(_BF16), norm2_gain[l].reshape(1, d_model))
        p1, e1, p2, e2, tau = _router(xn2, peer_w_q[l].astype(_BF16), peer_subkeys[l])
        h_all = _experts(xn2, peer_u[l].astype(_BF16), peer_v[l].astype(_BF16), p1, e1, p2, e2, tau, hp)

        k_all, v_all = proj[:, width:2 * width], proj[:, 2 * width:3 * width]
        outs[0].append(k_all[:tp].reshape(batch, seq, n_heads, HEAD_DIM))
        outs[1].append(v_all[:tp].reshape(batch, seq, n_heads, HEAD_DIM))
        outs[2].append(ret_p)
        outs[3].append(k_all[tp:].reshape(n_seq, n_new, n_heads, HEAD_DIM))
        outs[4].append(v_all[tp:].reshape(n_seq, n_new, n_heads, HEAD_DIM))
        outs[5].append(ret_s)
    return (h_all[:tp].reshape(batch, seq, d_model), h_all[tp:].reshape(n_seq, n_new, d_model),
            *(jnp.stack(o) for o in outs))
```

```python
import functools

import jax
import jax.numpy as jnp
import numpy as np
from jax import lax
from jax.experimental import pallas as pl
from jax.experimental.pallas import tpu as pltpu

HEAD_DIM = 128
NORM_EPS = 1e-6
ROPE_BASE = 10000.0
PEER_HEADS = 8
PEER_KEYS = 128
PEER_TOPK = 16
PEER_HALF = 128
QK_SCALE = HEAD_DIM ** -0.5

_F32 = jnp.float32
_BF16 = jnp.bfloat16
_NEG_INF = float("-inf")
_V7X_VMEM_LIMIT_BYTES = 56 * 1024 * 1024

_NT = (((1,), (1,)), ((), ()))


def _params(sem):
    return pltpu.CompilerParams(dimension_semantics=sem, vmem_limit_bytes=_V7X_VMEM_LIMIT_BYTES)


def _tile(n, cap, mult):
    best = None
    for d in range(mult, min(n, cap) + 1, mult):
        if n % d == 0:
            best = d
    assert best is not None, (n, cap, mult)
    return best


LOG2_E = 1.4426950408889634


def _softplus2(z2):
    return jnp.maximum(z2, 0.0) + jnp.log2(1.0 + jnp.exp2(-jnp.abs(z2)))


def _strict_lower_ones(n):
    row = lax.broadcasted_iota(jnp.int32, (n, n), 0)
    col = lax.broadcasted_iota(jnp.int32, (n, n), 1)
    return jnp.where(row > col, 1.0, 0.0).astype(_BF16)


def _inproj_body(x_ref, g_ref, w_ref, cos_ref, sin_ref, qg_ref, kg_ref, o_ref, xn_ref, *, n_heads):
    j = pl.program_id(1)

    @pl.when(j == 0)
    def _():
        x = x_ref[...]
        ms = jnp.mean(x * x, axis=-1, keepdims=True)
        xn_ref[...] = (x * lax.rsqrt(ms + NORM_EPS) * g_ref[...]).astype(_BF16)

    acc = jnp.dot(xn_ref[...], w_ref[...], preferred_element_type=_F32)

    def per_head(fn):
        for h in range(n_heads):
            sl = slice(h * HEAD_DIM, (h + 1) * HEAD_DIM)
            o_ref[:, sl] = fn(acc[:, sl], h)

    def rms(a, gain):
        return a * lax.rsqrt(jnp.mean(a * a, axis=-1, keepdims=True) + NORM_EPS) * gain

    def rope(a):
        return a * cos_ref[...] + pltpu.roll(a, HEAD_DIM // 2, 1) * sin_ref[...]

    @pl.when(j == 0)
    def _():
        per_head(lambda a, h: rms(a, qg_ref[h:h + 1, :]) * (QK_SCALE * LOG2_E))

    @pl.when(j == 1)
    def _():
        per_head(lambda a, h: rms(a, kg_ref[h:h + 1, :]))

    @pl.when(j == 3)
    def _():
        per_head(lambda a, h: rope(a))

    @pl.when(j == 4)
    def _():
        per_head(lambda a, h: rope(a) * QK_SCALE)

    @pl.when((j == 2) | (j == 5) | (j == 6))
    def _():
        o_ref[...] = acc


def _inproj(x, gain, w_bf16, cos_t, sin_t, q_gain, k_gain, n_heads):
    T, D = x.shape
    width = n_heads * HEAD_DIM
    n_seg = w_bf16.shape[1] // width
    tm = _tile(T, 640, 8)
    return pl.pallas_call(
        functools.partial(_inproj_body, n_heads=n_heads),
        out_shape=jax.ShapeDtypeStruct((T, n_seg * width), _F32),
        grid=(T // tm, n_seg),
        in_specs=[
            pl.BlockSpec((tm, D), lambda i, j: (i, 0)),
            pl.BlockSpec((1, D), lambda i, j: (0, 0)),
            pl.BlockSpec((D, width), lambda i, j: (0, j)),
            pl.BlockSpec((tm, HEAD_DIM), lambda i, j: (i, 0)),
            pl.BlockSpec((tm, HEAD_DIM), lambda i, j: (i, 0)),
            pl.BlockSpec((n_heads, HEAD_DIM), lambda i, j: (0, 0)),
            pl.BlockSpec((n_heads, HEAD_DIM), lambda i, j: (0, 0)),
        ],
        out_specs=pl.BlockSpec((tm, width), lambda i, j: (i, j)),
        scratch_shapes=[pltpu.VMEM((tm, D), _BF16)],
        compiler_params=_params(("arbitrary", "arbitrary")),
        name="inproj",
    )(x, gain, w_bf16, cos_t, sin_t, q_gain, k_gain)


def _sb_blocks(chains, tri):
    masked = lambda x, mask: x if mask is None else jnp.where(mask, x, 0.0)
    zs = [lax.dot_general(c[0], c[1], _NT, preferred_element_type=_F32) + c[3] for c in chains]
    sps = [masked(_softplus2(z), c[6]) for z, c in zip(zs, chains)]
    laters = [jnp.dot(sp.astype(_BF16), tri, preferred_element_type=_F32) for sp in sps]
    out = []
    for z, sp, later, (_, _, vs, _, carry, acc, mask) in zip(zs, sps, laters, chains):
        w = masked(jnp.exp2(z - sp - (carry + later)), mask)
        acc = acc + jnp.dot(w.astype(_BF16), vs, preferred_element_type=_F32)
        out.append((carry + jnp.sum(sp, axis=1, keepdims=True), acc))
    return out


def _sb_prompt_body(bias_ref, q_ref, k_ref, v_ref, gain_ref, o_ref, *, blk, heads):
    hg = pl.program_id(1)
    i = pl.program_id(2)
    tri = _strict_lower_ones(blk)
    row = lax.broadcasted_iota(jnp.int32, (blk, blk), 0)
    col = lax.broadcasted_iota(jnp.int32, (blk, blk), 1)
    causal = col < row
    cols = [slice(hh * HEAD_DIM, (hh + 1) * HEAD_DIM) for hh in range(heads)]
    biases = [bias_ref[hg * heads + hh] * LOG2_E for hh in range(heads)]
    q_lo = [q_ref[:blk, cl].astype(_BF16) for cl in cols]
    q_hi = [q_ref[blk:, cl].astype(_BF16) for cl in cols]

    def kv(kb):
        start = pl.multiple_of(kb * blk, blk)
        return ([k_ref[pl.ds(start, blk), cl].astype(_BF16) for cl in cols],
                [v_ref[pl.ds(start, blk), cl].astype(_BF16) for cl in cols])

    zero = (jnp.zeros((blk, 1), _F32), jnp.zeros((blk, HEAD_DIM), _F32))
    k1, v1 = kv(2 * i + 1)
    k0, v0 = kv(2 * i)
    state = _sb_blocks([(q_lo[hh], k0[hh], v0[hh], biases[hh], *zero, causal) for hh in range(heads)]
                       + [(q_hi[hh], k1[hh], v1[hh], biases[hh], *zero, causal) for hh in range(heads)], tri)
    lo, hi = state[:heads], state[heads:]
    hi = _sb_blocks([(q_hi[hh], k0[hh], v0[hh], biases[hh], *hi[hh], None) for hh in range(heads)], tri)

    def body(s, c):
        ks, vs = kv(2 * i - 1 - s)
        qs = q_lo + q_hi
        return tuple(_sb_blocks([(qs[n], ks[n % heads], vs[n % heads], biases[n % heads], *c[n], None)
                                 for n in range(2 * heads)], tri))

    state = lax.fori_loop(0, 2 * i, body, tuple(lo) + tuple(hi))
    for n, (_, acc) in enumerate(state):
        part, hh = divmod(n, heads)
        ms = jnp.mean(acc * acc, axis=-1, keepdims=True)
        o_ref[part * blk:(part + 1) * blk, cols[hh]] = (acc * lax.rsqrt(ms + NORM_EPS) * gain_ref[hh]).astype(
            o_ref.dtype)


def _sb_prompt(proj, sb_bias, sb_gain, batch, seq, n_heads):
    blk = _tile(seq // 2, 256, 128)
    nq = seq // (2 * blk)
    heads = _tile(n_heads, 2, 1)
    groups = n_heads // heads
    width = heads * HEAD_DIM
    return pl.pallas_call(
        functools.partial(_sb_prompt_body, blk=blk, heads=heads),
        out_shape=jax.ShapeDtypeStruct((batch * seq, n_heads * HEAD_DIM), _BF16),
        grid=(batch, groups, nq),
        in_specs=[
            pl.BlockSpec(memory_space=pltpu.SMEM),
            pl.BlockSpec((2 * blk, width), lambda b, g, i: (b * nq + i, g)),
            pl.BlockSpec((seq, width), lambda b, g, i: (b, groups + g)),
            pl.BlockSpec((seq, width), lambda b, g, i: (b, 2 * groups + g)),
            pl.BlockSpec((heads, 1, HEAD_DIM), lambda b, g, i: (g, 0, 0)),
        ],
        out_specs=pl.BlockSpec((2 * blk, width), lambda b, g, i: (b * nq + i, g)),
        compiler_params=_params(("arbitrary", "arbitrary", "arbitrary")),
        name="sb_prompt",
    )(sb_bias, proj, proj, proj, sb_gain.reshape(n_heads, 1, HEAD_DIM))


_LANES = 128
_MXU_COLS = 256


def _sb_flat_blocks(q, kfs, vfs, bias, tri_ones, carry, acc, valid):
    rows, groups = q.shape[0], kfs[0].shape[0] // _LANES
    zs = [lax.dot_general(q, kf, _NT, preferred_element_type=_F32) + bias for kf in kfs]
    sps = [jnp.where(valid, _softplus2(z), 0.0) for z in zs]
    boths = []
    for sp in sps:
        parts = [sp[:, g * _LANES:(g + 1) * _LANES] for g in range(groups)]
        stacked = parts[0] if groups == 1 else jnp.concatenate(parts, axis=0)
        boths.append(jnp.dot(stacked.astype(_BF16), tri_ones, preferred_element_type=_F32))
    ws = []
    for z, sp, both in zip(zs, sps, boths):
        later = [None] * groups
        for g in reversed(range(groups)):
            later[g] = both[g * rows:(g + 1) * rows, :_LANES] + carry
            carry = carry + both[g * rows:(g + 1) * rows, _LANES:]
        later = later[0] if groups == 1 else jnp.concatenate(later, axis=1)
        ws.append(jnp.where(valid, jnp.exp2(z - sp - later), 0.0).astype(_BF16))
    for w, vf in zip(ws, vfs):
        acc = acc + jnp.dot(w, vf, preferred_element_type=_F32)
    return carry, acc


def _sb_decode_body(pt_ref, q_ref, bias_ref, tq_ref, hq_ref, hk_ref, hkn_ref, keyn_ref, knew_ref, vnew_ref,
                    gain_ref, *rest, pages_per_step):
    del pt_ref
    kv_refs = rest[:2 * pages_per_step]
    o_ref, carry_ref, acc_ref = rest[2 * pages_per_step:]
    s = pl.program_id(1)
    q = q_ref[...].astype(_BF16)
    bias = bias_ref[...]
    row = lax.broadcasted_iota(jnp.int32, (_LANES, 2 * _LANES), 0)
    col = lax.broadcasted_iota(jnp.int32, (_LANES, 2 * _LANES), 1)
    tri_ones = jnp.where(col >= _LANES, 1.0, jnp.where(row > col, 1.0, 0.0)).astype(_BF16)
    own = hq_ref[...] == hk_ref[...]

    @pl.when(s == 0)
    def _():
        never = jnp.iinfo(jnp.int32).max
        valid = jnp.where(hq_ref[...] == hkn_ref[...], keyn_ref[...], never) < tq_ref[...]
        carry, acc = _sb_flat_blocks(q, [knew_ref[...].astype(_BF16)], [vnew_ref[...].astype(_BF16)], bias,
                                     tri_ones, jnp.zeros(carry_ref.shape, _F32), jnp.zeros(acc_ref.shape, _F32),
                                     valid)
        carry_ref[...] = carry
        acc_ref[...] = acc

    def flat(ref):
        page = ref[...]
        return page.reshape(page.shape[0] * page.shape[1], page.shape[2]).astype(_BF16)

    carry, acc = _sb_flat_blocks(q, [flat(r) for r in kv_refs[0::2]], [flat(r) for r in kv_refs[1::2]], bias,
                                 tri_ones, carry_ref[...], acc_ref[...], own)
    carry_ref[...] = carry
    acc_ref[...] = acc

    @pl.when(s == pl.num_programs(1) - 1)
    def _():
        ms = jnp.mean(acc * acc, axis=-1, keepdims=True)
        o_ref[...] = acc * lax.rsqrt(ms + NORM_EPS) * gain_ref[...]


def _sb_decode(q_rows, k_new, v_new, sb_bias, sb_gain, cache_k, cache_v, layer, page_table, n_new):
    n_seq, n_rows, hd = q_rows.shape
    page, n_heads = cache_k.shape[2], cache_k.shape[3]
    n_pages = page_table.shape[1]
    pps = _tile(n_pages, 16, 1)
    n_steps = n_pages // pps
    assert n_rows <= _LANES and _LANES % n_heads == 0

    i32 = jnp.int32
    col = lambda v: v.reshape(n_rows, 1)
    bias_col = col(jnp.tile(sb_bias, n_new)) * LOG2_E
    tq_col = col(jnp.repeat(jnp.arange(n_new, dtype=i32), n_heads))
    hq_col = col(jnp.tile(jnp.arange(n_heads, dtype=i32), n_new))
    hk_row = jnp.tile(jnp.arange(n_heads, dtype=i32), page).reshape(1, page * n_heads)
    hkn_row = jnp.tile(jnp.arange(n_heads, dtype=i32), _LANES // n_heads).reshape(1, _LANES)
    keyn_row = jnp.repeat(jnp.arange(_LANES // n_heads, dtype=i32), n_heads).reshape(1, _LANES)
    gain_rows = jnp.tile(sb_gain, (n_new, 1))
    pad = lambda t: jnp.pad(t, ((0, 0), (0, _LANES - n_rows), (0, 0)))

    def page_map(r):
        return lambda b, s, pt: (layer, pt[b, n_pages - 1 - (s * pps + r)], 0, 0, 0)

    kv_specs, kv_args = [], []
    for r in range(pps):
        kv_specs += [pl.BlockSpec((None, None, page, n_heads, hd), page_map(r))] * 2
        kv_args += [cache_k, cache_v]
    seq_map = lambda b, s, pt: (b, 0, 0)
    fix_map = lambda b, s, pt: (0, 0)
    fixed = lambda a: pl.BlockSpec(a.shape, fix_map)
    small = [bias_col, tq_col, hq_col, hk_row, hkn_row, keyn_row]
    grid_spec = pltpu.PrefetchScalarGridSpec(
        num_scalar_prefetch=1,
        grid=(n_seq, n_steps),
        in_specs=[pl.BlockSpec((None, n_rows, hd), seq_map)] + [fixed(a) for a in small] + [
            pl.BlockSpec((None, _LANES, hd), seq_map),
            pl.BlockSpec((None, _LANES, hd), seq_map),
            fixed(gain_rows),
        ] + kv_specs,
        out_specs=pl.BlockSpec((None, n_rows, hd), seq_map),
        scratch_shapes=[pltpu.VMEM((n_rows, _LANES), _F32), pltpu.VMEM((n_rows, hd), _F32)],
    )
    return pl.pallas_call(
        functools.partial(_sb_decode_body, pages_per_step=pps),
        out_shape=jax.ShapeDtypeStruct((n_seq, n_rows, hd), _F32),
        grid_spec=grid_spec,
        compiler_params=_params(("arbitrary", "arbitrary")),
        name="sb_decode",
    )(page_table, q_rows, *small, pad(k_new), pad(v_new), gain_rows, *kv_args)


def _gated_group_norm(o, gain, gate):
    mu = jnp.mean(o, axis=-1, keepdims=True)
    d = o - mu
    var = jnp.mean(d * d, axis=-1, keepdims=True)
    return d * lax.rsqrt(var + NORM_EPS) * gain * (gate * jax.nn.sigmoid(gate))


def _ret_prompt_body(logg_ref, q_ref, k_ref, v_ref, g_ref, gain_ref, o_ref, s_ref, state_ref, decay_ref, *, chunk,
                     heads):
    hg = pl.program_id(1)
    c = pl.program_id(2)

    lgs = [logg_ref[hg * heads + i] for i in range(heads)]

    @pl.when(c == 0)
    def _():
        state_ref[...] = jnp.zeros_like(state_ref)
        li = lax.broadcasted_iota(jnp.int32, (chunk, chunk), 0)
        mi = lax.broadcasted_iota(jnp.int32, (chunk, chunk), 1)
        dist = (li - mi).astype(_F32)
        for i in range(heads):
            decay_ref[i] = jnp.where(dist >= 0, jnp.exp(jnp.maximum(dist, 0.0) * lgs[i]), 0.0)

    pos = lax.broadcasted_iota(jnp.int32, (chunk, 1), 0).astype(_F32)
    full = jnp.full((1, HEAD_DIM), float(chunk), _F32)
    cols = [slice(i * HEAD_DIM, (i + 1) * HEAD_DIM) for i in range(heads)]
    qs = [q_ref[:, cl].astype(_BF16) for cl in cols]
    ks = [k_ref[:, cl] for cl in cols]
    vs = [v_ref[:, cl].astype(_BF16) for cl in cols]
    states = [state_ref[i] for i in range(heads)]
    scores = [lax.dot_general(q, k.astype(_BF16), _NT, preferred_element_type=_F32) for q, k in zip(qs, ks)]
    from_state = [jnp.dot(q, s.astype(_BF16), preferred_element_type=_F32) for q, s in zip(qs, states)]
    updates = [jnp.dot((k * jnp.exp((chunk - 1.0 - pos) * lg)).T.astype(_BF16), v, preferred_element_type=_F32)
               for k, v, lg in zip(ks, vs, lgs)]
    decayed = [(sc * decay_ref[i]).astype(_BF16) for i, sc in enumerate(scores)]
    outs = [jnp.dot(d, v, preferred_element_type=_F32) for d, v in zip(decayed, vs)]
    last = c == pl.num_programs(2) - 1
    for i in range(heads):
        new_state = states[i] * jnp.exp(full * lgs[i]) + updates[i]
        state_ref[i] = new_state

        @pl.when(last)
        def _():
            s_ref[i] = new_state

        o = outs[i] + from_state[i] * jnp.exp((pos + 1.0) * lgs[i])
        o_ref[:, cols[i]] = _gated_group_norm(o, gain_ref[i], g_ref[:, cols[i]]).astype(o_ref.dtype)


def _ret_prompt(proj, log_g, ret_gain, batch, seq, n_heads):
    chunk = _tile(seq, 256, 128)
    nc = seq // chunk
    heads = _tile(n_heads, 4, 1)
    groups = n_heads // heads
    width = heads * HEAD_DIM
    col = lambda seg: (lambda b, hg, c: (b * nc + c, seg * groups + hg))
    return pl.pallas_call(
        functools.partial(_ret_prompt_body, chunk=chunk, heads=heads),
        out_shape=(jax.ShapeDtypeStruct((batch * seq, n_heads * HEAD_DIM), _BF16),
                   jax.ShapeDtypeStruct((batch, n_heads, HEAD_DIM, HEAD_DIM), _F32)),
        grid=(batch, groups, nc),
        in_specs=[
            pl.BlockSpec(memory_space=pltpu.SMEM),
            pl.BlockSpec((chunk, width), col(3)),
            pl.BlockSpec((chunk, width), col(4)),
            pl.BlockSpec((chunk, width), col(5)),
            pl.BlockSpec((chunk, width), col(6)),
            pl.BlockSpec((heads, 1, HEAD_DIM), lambda b, hg, c: (hg, 0, 0)),
        ],
        out_specs=(pl.BlockSpec((chunk, width), lambda b, hg, c: (b * nc + c, hg)),
                   pl.BlockSpec((None, heads, HEAD_DIM, HEAD_DIM), lambda b, hg, c: (b, hg, 0, 0))),
        scratch_shapes=[pltpu.VMEM((heads, HEAD_DIM, HEAD_DIM), _F32), pltpu.VMEM((heads, chunk, chunk), _F32)],
        compiler_params=_params(("arbitrary", "arbitrary", "arbitrary")),
        name="ret_prompt",
    )(log_g, proj, proj, proj, proj, ret_gain.reshape(n_heads, 1, HEAD_DIM))


def _ret_sample_body(logg_ref, q_ref, k_ref, v_ref, g_ref, gain_ref, seqc_ref, seqr_ref, posc_ref, posr_ref,
                     s0_ref, o_ref, s_ref, *, n_new):
    h = pl.program_id(0)
    lg = logg_ref[h]
    n_seq = s0_ref.shape[0]
    q = q_ref[...].astype(_BF16)
    k = k_ref[...]
    v = v_ref[...].astype(_BF16)
    seq_c, seq_r = seqc_ref[...], seqr_ref[...]
    pos_c, pos_r = posc_ref[...], posr_ref[...]
    dist = pos_c - pos_r
    decay = jnp.where(seq_c == seq_r, jnp.where(dist >= 0, jnp.exp(jnp.maximum(dist, 0.0) * lg), 0.0), 0.0)
    scores = lax.dot_general(q, k.astype(_BF16), _NT, preferred_element_type=_F32) * decay
    o = jnp.dot(scores.astype(_BF16), v, preferred_element_type=_F32)
    q_decay = jnp.exp((pos_c + 1.0) * lg)
    k_tail_t = k.T * jnp.exp((n_new - 1.0 - pos_r) * lg)
    full = jnp.full((1, HEAD_DIM), float(n_new), _F32)
    state_decay = jnp.exp(full * lg)

    def body(b, o):
        state = s0_ref[b]
        from_state = jnp.dot(q, state.astype(_BF16), preferred_element_type=_F32) * q_decay
        o = o + jnp.where(seq_c == b, from_state, 0.0)
        kt = jnp.where(seq_r == b, k_tail_t, 0.0).astype(_BF16)
        s_ref[b] = state * state_decay + jnp.dot(kt, v, preferred_element_type=_F32)
        return o

    o = lax.fori_loop(0, n_seq, body, o)
    o_ref[...] = _gated_group_norm(o, gain_ref[...], g_ref[...]).astype(o_ref.dtype)


def _ret_sample(proj_s, log_g, ret_gain, state, n_seq, n_new, n_heads):
    ts = n_seq * n_new
    seq_id = jnp.repeat(jnp.arange(n_seq, dtype=jnp.int32), n_new)
    pos = jnp.tile(jnp.arange(n_new, dtype=_F32), n_seq)
    col = lambda seg: (lambda h: (0, seg * n_heads + h))
    fix = lambda h: (0, 0)
    return pl.pallas_call(
        functools.partial(_ret_sample_body, n_new=n_new),
        out_shape=(jax.ShapeDtypeStruct((ts, n_heads * HEAD_DIM), _BF16),
                   jax.ShapeDtypeStruct((n_seq, n_heads, HEAD_DIM, HEAD_DIM), _F32)),
        grid=(n_heads,),
        in_specs=[
            pl.BlockSpec(memory_space=pltpu.SMEM),
            pl.BlockSpec((ts, HEAD_DIM), col(3)),
            pl.BlockSpec((ts, HEAD_DIM), col(4)),
            pl.BlockSpec((ts, HEAD_DIM), col(5)),
            pl.BlockSpec((ts, HEAD_DIM), col(6)),
            pl.BlockSpec((None, 1, HEAD_DIM), lambda h: (h, 0, 0)),
            pl.BlockSpec((ts, 1), fix),
            pl.BlockSpec((1, ts), fix),
            pl.BlockSpec((ts, 1), fix),
            pl.BlockSpec((1, ts), fix),
            pl.BlockSpec((n_seq, None, HEAD_DIM, HEAD_DIM), lambda h: (0, h, 0, 0)),
        ],
        out_specs=(pl.BlockSpec((ts, HEAD_DIM), lambda h: (0, h)),
                   pl.BlockSpec((n_seq, None, HEAD_DIM, HEAD_DIM), lambda h: (0, h, 0, 0))),
        compiler_params=_params(("arbitrary",)),
        name="ret_sample",
    )(log_g, proj_s, proj_s, proj_s, proj_s, ret_gain.reshape(n_heads, 1, HEAD_DIM),
      seq_id.reshape(ts, 1), seq_id.reshape(1, ts), pos.reshape(ts, 1), pos.reshape(1, ts), state)


def _outproj_body(oa_ref, ob_ref, x_ref, w_ref, g_ref, hp_ref, xn_ref):
    half = oa_ref.shape[1]
    hp = (x_ref[...]
          + jnp.dot(oa_ref[...], w_ref[:half, :], preferred_element_type=_F32)
          + jnp.dot(ob_ref[...], w_ref[half:, :], preferred_element_type=_F32))
    hp_ref[...] = hp
    ms = jnp.mean(hp * hp, axis=-1, keepdims=True)
    xn_ref[...] = (hp * lax.rsqrt(ms + NORM_EPS) * g_ref[...]).astype(xn_ref.dtype)


def _outproj(o_a, o_b, x, w_bf16, gain2):
    T, D = x.shape
    half = o_a.shape[1]
    tm = _tile(T, 320, 16)
    return pl.pallas_call(
        _outproj_body,
        out_shape=(jax.ShapeDtypeStruct((T, D), _F32), jax.ShapeDtypeStruct((T, D), _BF16)),
        grid=(T // tm,),
        in_specs=[
            pl.BlockSpec((tm, half), lambda i: (i, 0)),
            pl.BlockSpec((tm, half), lambda i: (i, 0)),
            pl.BlockSpec((tm, D), lambda i: (i, 0)),
            pl.BlockSpec((2 * half, D), lambda i: (0, 0)),
            pl.BlockSpec((1, D), lambda i: (0, 0)),
        ],
        out_specs=(pl.BlockSpec((tm, D), lambda i: (i, 0)), pl.BlockSpec((tm, D), lambda i: (i, 0))),
        compiler_params=_params(("arbitrary",)),
        name="outproj",
    )(o_a, o_b, x, w_bf16, gain2)


_SUBLANES = 8


def _merge_sort_network(lo, hi):
    def merge(lo, hi, r):
        step = 2 * r
        if step < hi - lo:
            yield from merge(lo, hi, step)
            yield from merge(lo + r, hi, step)
            yield from ((i, i + r) for i in range(lo + r, hi - r, step))
        else:
            yield (lo, lo + r)

    if hi > lo:
        mid = lo + (hi - lo) // 2
        yield from _merge_sort_network(lo, mid)
        yield from _merge_sort_network(mid + 1, hi)
        yield from merge(lo, hi, 1)


def _top_values(s, n):
    assert s.shape[0] == n * _SUBLANES and n & (n - 1) == 0

    def exchange(v, i, j):
        v[i], v[j] = jnp.maximum(v[i], v[j]), jnp.minimum(v[i], v[j])

    slabs = [s[i * _SUBLANES:(i + 1) * _SUBLANES, :] for i in range(n)]
    for i, j in _merge_sort_network(0, n - 1):
        exchange(slabs, i, j)
    shift = _SUBLANES // 2
    while shift:
        slabs = [jnp.maximum(slabs[k], pltpu.roll(slabs[n - 1 - k], shift, 0)) for k in range(n)]
        d = n // 2
        while d:
            for k in range(n):
                if k & d == 0:
                    exchange(slabs, k, k + d)
            d //= 2
        shift //= 2
    return [v[0:1, :] for v in slabs]


def _stack_rows(rows):
    n = len(rows)
    ridx = lax.broadcasted_iota(jnp.int32, (n, rows[0].shape[1]), 0)
    out = jnp.broadcast_to(rows[0], ridx.shape)
    for i in range(1, n):
        out = jnp.where(ridx == i, rows[i], out)
    return out


def _router_body(xn_ref, wq_ref, sk_ref, p1_ref, e1_ref, p2_ref, e2_ref, tau_ref, q_ref):
    k = PEER_TOPK
    q_ref[...] = jnp.dot(xn_ref[...], wq_ref[...], preferred_element_type=_F32)
    tt = xn_ref.shape[0]
    ridx = lax.broadcasted_iota(jnp.int32, (k, tt), 0)

    def head(h, _):
        def scores(c):
            start = pl.multiple_of((2 * h + c) * PEER_HALF, PEER_HALF)
            qc = q_ref[:, pl.ds(start, PEER_HALF)]
            sk = sk_ref[h, c]
            q_hi = qc.astype(_BF16)
            q_lo = (qc - q_hi.astype(_F32)).astype(_BF16)
            sk_hi = sk.astype(_BF16)
            sk_lo = (sk - sk_hi.astype(_F32)).astype(_BF16)
            dot = lambda a, b: lax.dot_general(a, b, _NT, preferred_element_type=_F32)
            return dot(sk_hi, q_hi) + (dot(sk_hi, q_lo) + dot(sk_lo, q_hi))

        s1, s2 = scores(0), scores(1)
        t1, t2 = _top_values(s1, k), _top_values(s2, k)
        t1s, t2s = _stack_rows(t1), _stack_rows(t2)
        cands = [jnp.where(ridx < k // (i + 1), t1[i] + t2s, _NEG_INF) for i in range(k // 2)]
        cands.append(jnp.where(ridx >= k // 2, t1s + t2[0], _NEG_INF))
        top = t1[0] + t2[0]
        z = jnp.zeros_like(top)
        tau = top
        for _ in range(k):
            m = cands[0]
            for cnd in cands[1:]:
                m = jnp.maximum(m, cnd)
            m = jnp.max(m, axis=0, keepdims=True)
            z = z + jnp.exp(m - top)
            tau = m
            cands = [jnp.where(cnd == m, _NEG_INF, cnd) for cnd in cands]
        p1_ref[h] = jnp.where(s1 >= t1[k - 1], s1, _NEG_INF)
        p2_ref[h] = jnp.where(s2 >= t2[k - 1], s2, _NEG_INF)
        e1_ref[h] = jnp.exp(s1 - t1[0])
        e2_ref[h] = jnp.exp(s2 - t2[0]) / z
        tau_ref[h] = tau
        return 0

    def head_pair(i, _):
        head(2 * i, 0)
        return head(2 * i + 1, 0)

    lax.fori_loop(0, PEER_HEADS // 2, head_pair, 0)


def _router(xn, wq_bf16, subkeys):
    T, D = xn.shape
    tt = _tile(T, 640, 128)
    big = jax.ShapeDtypeStruct((PEER_HEADS, PEER_KEYS, T), _F32)
    big_spec = pl.BlockSpec((PEER_HEADS, PEER_KEYS, tt), lambda i: (0, 0, i))
    return pl.pallas_call(
        _router_body,
        out_shape=(big, big, big, big, jax.ShapeDtypeStruct((PEER_HEADS, 1, T), _F32)),
        grid=(T // tt,),
        in_specs=[
            pl.BlockSpec((tt, D), lambda i: (i, 0)),
            pl.BlockSpec(wq_bf16.shape, lambda i: (0, 0)),
            pl.BlockSpec(subkeys.shape, lambda i: (0, 0, 0, 0)),
        ],
        out_specs=(big_spec, big_spec, big_spec, big_spec,
                   pl.BlockSpec((PEER_HEADS, 1, tt), lambda i: (0, 0, i))),
        scratch_shapes=[pltpu.VMEM((tt, wq_bf16.shape[1]), _F32)],
        compiler_params=_params(("arbitrary",)),
        name="peer_router",
    )(xn, wq_bf16, subkeys)


def _gelu_tanh(x):
    c = 0.7978845608028654
    return x * (0.5 * (1.0 + jnp.tanh(c * (x + 0.044715 * (x * x * x)))))


def _experts_step(xn_ref, u_ref, v_ref, p1_ref, e1_ref, p2_ref, e2_ref, tau_ref, y_ref, act_ref, hid_in, hid_out,
                  first_a, a_per_tile):
    tt = y_ref.shape[0]
    for half in range(a_per_tile // 2):
        cols = slice(half * _MXU_COLS, (half + 1) * _MXU_COLS)
        for aa in range(2 * half, 2 * half + 2):
            a = first_a + aa
            acols = slice(aa * PEER_KEYS, (aa + 1) * PEER_KEYS)
            p1_rows = [p1_ref[h, pl.ds(a, 1), :] for h in range(PEER_HEADS)]
            e1_rows = [e1_ref[h, pl.ds(a, 1), :] for h in range(PEER_HEADS)]
            for tg in range(tt // _LANES):
                toks = slice(tg * _LANES, (tg + 1) * _LANES)
                gate_t = jnp.zeros((PEER_KEYS, _LANES), _F32)
                for h in range(PEER_HEADS):
                    cand = p1_rows[h][:, toks] + p2_ref[h, :, toks]
                    pair = e1_rows[h][:, toks] * e2_ref[h, :, toks]
                    gate_t = gate_t + jnp.where(cand >= tau_ref[h, :, toks], pair, 0.0)
                act_ref[toks, acols] = (_gelu_tanh(hid_in[toks, acols]) * gate_t.T).astype(act_ref.dtype)
        y_ref[...] += jnp.dot(act_ref[:, cols], v_ref[cols, :], preferred_element_type=_F32)
    hid_out[...] = lax.dot_general(xn_ref[...], u_ref[...], _NT, preferred_element_type=_F32)


def _experts_body(xn_ref, u_ref, v_ref, p1_ref, e1_ref, p2_ref, e2_ref, tau_ref, hp_ref, y_ref, act_ref, hid_a,
                  hid_b, *, a_per_tile):
    j = pl.program_id(1)
    first_a = (j - 1) * a_per_tile
    step = functools.partial(_experts_step, xn_ref, u_ref, v_ref, p1_ref, e1_ref, p2_ref, e2_ref, tau_ref, y_ref,
                             act_ref)

    @pl.when(j == 0)
    def _():
        y_ref[...] = hp_ref[...]
        hid_a[...] = lax.dot_general(xn_ref[...], u_ref[...], _NT, preferred_element_type=_F32)

    @pl.when(j % 2 == 1)
    def _():
        step(hid_a, hid_b, first_a, a_per_tile)

    @pl.when((j % 2 == 0) & (j > 0))
    def _():
        step(hid_b, hid_a, first_a, a_per_tile)


def _experts(xn, u_bf16, v_bf16, p1, e1, p2, e2, tau, hp):
    T, D = xn.shape
    n_experts = u_bf16.shape[0]
    tt = _tile(T, 640, 128)
    a_per_tile = 8
    et = a_per_tile * PEER_KEYS
    assert et % _MXU_COLS == 0 and n_experts % et == 0
    n_tiles = n_experts // et
    once = pl.Buffered(1)
    tok = pl.BlockSpec((PEER_HEADS, PEER_KEYS, tt), lambda i, j: (0, 0, i), pipeline_mode=once)
    return pl.pallas_call(
        functools.partial(_experts_body, a_per_tile=a_per_tile),
        out_shape=jax.ShapeDtypeStruct((T, D), _F32),
        grid=(T // tt, n_tiles + 1),
        in_specs=[
            pl.BlockSpec((tt, D), lambda i, j: (i, 0), pipeline_mode=once),
            pl.BlockSpec((et, D), lambda i, j: (jnp.minimum(j, n_tiles - 1), 0)),
            pl.BlockSpec((et, D), lambda i, j: (jnp.maximum(j - 1, 0), 0)),
            tok, tok, tok, tok,
            pl.BlockSpec((PEER_HEADS, 1, tt), lambda i, j: (0, 0, i)),
            pl.BlockSpec((tt, D), lambda i, j: (i, 0), pipeline_mode=once),
        ],
        out_specs=pl.BlockSpec((tt, D), lambda i, j: (i, 0)),
        scratch_shapes=[pltpu.VMEM((tt, et), _BF16), pltpu.VMEM((tt, et), _F32), pltpu.VMEM((tt, et), _F32)],
        compiler_params=_params(("arbitrary", "arbitrary")),
        name="peer_experts",
    )(xn, u_bf16, v_bf16, p1, e1, p2, e2, tau, hp)


def _rope_tables(pos):
    half = HEAD_DIM // 2
    freqs = ROPE_BASE ** (-jnp.arange(half, dtype=_F32) / half)
    ang = jnp.asarray(pos, _F32)[:, None] * freqs[None, :]
    cos, sin = jnp.cos(ang), jnp.sin(ang)
    return jnp.concatenate([cos, cos], axis=-1), jnp.concatenate([-sin, sin], axis=-1)


def kernel(x_prompt, x_sample, cache_k, cache_v, state_ret, page_table, norm1_gain, w_in, q_norm_gain,
           k_norm_gain, sb_bias, sb_out_gain, ret_out_gain, w_out, norm2_gain, peer_w_q, peer_subkeys,
           peer_u, peer_v):
    batch, seq, d_model = x_prompt.shape
    n_seq, n_new, _ = x_sample.shape
    depth, page, n_heads = cache_k.shape[0], cache_k.shape[2], cache_k.shape[3]
    width = n_heads * HEAD_DIM
    assert w_in.shape[2] == 7 * width and w_out.shape[1] == 2 * width
    assert n_heads * n_new % 8 == 0 and seq % (n_seq * n_new) == 0
    tp, ts = batch * seq, n_seq * n_new
    past = page_table.shape[1] * page

    pos = np.concatenate([np.tile(np.arange(seq), batch), np.tile(past + np.arange(n_new), n_seq)])
    cos_t, sin_t = _rope_tables(pos)
    log_g = jnp.log1p(-jnp.power(2.0, -5.0 - jnp.arange(n_heads, dtype=_F32)))

    h_all = jnp.concatenate([x_prompt.reshape(tp, d_model), x_sample.reshape(ts, d_model)], axis=0)
    outs = [[] for _ in range(6)]
    for l in range(depth):
        proj = _inproj(h_all, norm1_gain[l].reshape(1, d_model), w_in[l].astype(_BF16), cos_t, sin_t,
                       q_norm_gain[l], k_norm_gain[l], n_heads)
        proj_s = proj[tp:]
        oa_p = _sb_prompt(proj, sb_bias[l], sb_out_gain[l], batch, seq, n_heads)
        ob_p, ret_p = _ret_prompt(proj, log_g, ret_out_gain[l], batch, seq, n_heads)
        rows = lambda seg: proj_s[:, seg * width:(seg + 1) * width].reshape(n_seq, n_new * n_heads, HEAD_DIM)
        oa_s = _sb_decode(rows(0), rows(1), rows(2), sb_bias[l], sb_out_gain[l], cache_k, cache_v, l,
                          page_table, n_new)
        ob_s, ret_s = _ret_sample(proj_s, log_g, ret_out_gain[l], state_ret[l], n_seq, n_new, n_heads)
        o_a = jnp.concatenate([oa_p, oa_s.reshape(ts, width).astype(_BF16)], axis=0)
        o_b = jnp.concatenate([ob_p, ob_s], axis=0)
        hp, xn2 = _outproj(o_a, o_b, h_all, w_out[l].astype(_BF16), norm2_gain[l].reshape(1, d_model))
        p1, e1, p2, e2, tau = _router(xn2, peer_w_q[l].astype(_BF16), peer_subkeys[l])
        h_all = _experts(xn2, peer_u[l].astype(_BF16), peer_v[l].astype(_BF16), p1, e1, p2, e2, tau, hp)

        k_all, v_all = proj[:, width:2 * width], proj[:, 2 * width:3 * width]
        outs[0].append(k_all[:tp].reshape(batch, seq, n_heads, HEAD_DIM))
        outs[1].append(v_all[:tp].reshape(batch, seq, n_heads, HEAD_DIM))
        outs[2].append(ret_p)
        outs[3].append(k_all[tp:].reshape(n_seq, n_new, n_heads, HEAD_DIM))
        outs[4].append(v_all[tp:].reshape(n_seq, n_new, n_heads, HEAD_DIM))
        outs[5].append(ret_s)
    return (h_all[:tp].reshape(batch, seq, d_model), h_all[tp:].reshape(n_seq, n_new, d_model),
            *(jnp.stack(o) for o in outs))
```
